```python
import math
import jax, jax.numpy as jnp
from jax import lax
import numpy as np

D_MODEL = 1024
BATCH = 16
SEQ = 2048
DEPTH = 1

DN_HEADS = 8
DN_HEAD_DIM = 128
DN_QK = DN_HEADS * DN_HEAD_DIM
DN_V = DN_HEADS * DN_HEAD_DIM
DN_CONV = 4
DN_CHUNK = 64
DIL_GROUPS = ((128, 1), (512, 4), (2048, 16))
DIL_HEADS_PER_GROUP = 4
DIL_HEAD_DIM = 128
DIL_HEADS = len(DIL_GROUPS) * DIL_HEADS_PER_GROUP
DIL_W = DIL_HEADS * DIL_HEAD_DIM
DIL_OUT = DIL_HEADS_PER_GROUP * DIL_HEAD_DIM
DIL_BLOCK = 128
ROPE_THETA = 10000.0
MEM_TOKENS = 256
MEM_HEADS = 4
MEM_HEAD_DIM = D_MODEL // MEM_HEADS
D_FF = -(-8 * D_MODEL // (3 * 256)) * 256
IN_SIZES = (2 * DN_QK + DN_V, DN_V, DN_HEADS, DN_HEADS, DIL_W, DIL_W, DIL_W, D_MODEL, D_MODEL)
IN_WIDTH = sum(IN_SIZES)
DEEPNORM_ALPHA = (2.0 * DEPTH) ** 0.25
DEEPNORM_BETA = (8.0 * DEPTH) ** -0.25
LN_EPS = 1e-5
RMS_EPS = 1e-6

kernel_name = 'hybrid_deltanet_dilated_attn_block'


def layer_norm(x, g, b):
    xf = x.astype(jnp.float32)
    mu = jnp.mean(xf, -1, keepdims=True)
    var = jnp.mean(jnp.square(xf - mu), -1, keepdims=True)
    return ((xf - mu) * lax.rsqrt(var + LN_EPS) * g + b).astype(x.dtype)


def l2_normalize(x):
    return x * lax.rsqrt(jnp.sum(jnp.square(x), -1, keepdims=True) + RMS_EPS)


def causal_depthwise_conv(x, w):
    K = w.shape[0]
    S = x.shape[1]
    xp = jnp.pad(x, ((0, 0), (K - 1, 0), (0, 0)))
    out = xp[:, K - 1:K - 1 + S] * w[K - 1]
    for j in range(K - 1):
        out = out + xp[:, j:j + S] * w[j]
    return out


def rope(x, positions):
    D = x.shape[-1]
    inv_freq = ROPE_THETA ** (-jnp.arange(0, D, 2, dtype=jnp.float32) / D)
    ang = positions.astype(jnp.float32)[..., None] * inv_freq
    cos = jnp.cos(ang)[:, :, None]
    sin = jnp.sin(ang)[:, :, None]
    xf = x.astype(jnp.float32)
    x1, x2 = xf[..., :D // 2], xf[..., D // 2:]
    return jnp.concatenate([x1 * cos - x2 * sin, x2 * cos + x1 * sin], -1).astype(x.dtype)


def unit_lower_inverse(n):
    C = n.shape[-1]
    t = jnp.eye(C, dtype=n.dtype) + n
    p = n
    for _ in range(int(math.log2(C)) - 1):
        p = p @ p
        t = t + t @ p
    return t


def gated_delta_rule_chunked(q, k, v, g, beta):
    B, S, H, Dk = q.shape
    Dv = v.shape[-1]
    C = DN_CHUNK
    nc = S // C

    def chunks(t):
        return t.reshape(B, nc, C, H, -1).transpose(0, 3, 1, 2, 4)

    q, k, v = chunks(q), chunks(k), chunks(v)
    g = g.reshape(B, nc, C, H).transpose(0, 3, 1, 2)
    beta = beta.reshape(B, nc, C, H).transpose(0, 3, 1, 2)
    gam = jnp.cumsum(g, axis=-1)
    idx = jnp.arange(C)
    incl = idx[:, None] >= idx[None, :]
    strict = idx[:, None] > idx[None, :]
    dec_incl = jnp.exp(jnp.where(incl, gam[..., :, None] - gam[..., None, :], -jnp.inf))
    dec_strict = jnp.where(strict, dec_incl, 0.0)
    kb = k * beta[..., None]
    n = -jnp.einsum('bhncd,bhnsd->bhncs', kb, k) * dec_strict
    t = unit_lower_inverse(n)
    u = jnp.einsum('bhncs,bhnsd->bhncd', t, v * beta[..., None])
    w = jnp.einsum('bhncs,bhnsd->bhncd', t, kb * jnp.exp(gam)[..., None])
    qk = jnp.einsum('bhncd,bhnsd->bhncs', q, k) * dec_incl
    q_dec = q * jnp.exp(gam)[..., None]
    k_dec = k * jnp.exp(gam[..., -1:] - gam)[..., None]
    g_tot = jnp.exp(gam[..., -1])

    def step(state, xs):
        u_n, w_n, qk_n, qd_n, kd_n, gt_n = xs
        v_new = u_n - jnp.einsum('bhcd,bhde->bhce', w_n, state)
        o_n = jnp.einsum('bhcd,bhde->bhce', qd_n, state) + jnp.einsum('bhcs,bhse->bhce', qk_n, v_new)
        state = gt_n[..., None, None] * state + jnp.einsum('bhcd,bhce->bhde', kd_n, v_new)
        return state, o_n

    xs = (jnp.moveaxis(u, 2, 0), jnp.moveaxis(w, 2, 0), jnp.moveaxis(qk, 2, 0),
          jnp.moveaxis(q_dec, 2, 0), jnp.moveaxis(k_dec, 2, 0), jnp.moveaxis(g_tot, 2, 0))
    state0 = jnp.zeros((B, H, Dk, Dv), q.dtype)
    _, o = lax.scan(step, state0, xs)
    return o.transpose(1, 0, 3, 2, 4).reshape(B, S, H, Dv)


def banded_attention(q, k, v, window):
    Bt, L, H, D = q.shape
    blk = DIL_BLOCK
    nb = -(-L // blk)
    pad = nb * blk - L
    qb = jnp.pad(q, ((0, 0), (0, pad), (0, 0), (0, 0))).reshape(Bt, nb, blk, H, D)

    def key_blocks(t):
        t = jnp.pad(t, ((0, 0), (blk, pad), (0, 0), (0, 0))).reshape(Bt, nb + 1, blk, H, D)
        return jnp.concatenate([t[:, :-1], t[:, 1:]], axis=2)

    kb, vb = key_blocks(k), key_blocks(v)
    qpos = jnp.arange(nb)[:, None, None] * blk + jnp.arange(blk)[None, :, None]
    kpos = jnp.arange(nb)[:, None, None] * blk - blk + jnp.arange(2 * blk)[None, None, :]
    dist = qpos - kpos
    mask = (dist >= 0) & (dist <= window) & (kpos >= 0)
    s = jnp.einsum('bnqhd,bnkhd->bnhqk', qb, kb).astype(jnp.float32) * (D ** -0.5)
    s = jnp.where(mask[None, :, None], s, -jnp.inf)
    m = jnp.max(s, -1, keepdims=True)
    p = jnp.exp(s - m)
    l = jnp.sum(p, -1, keepdims=True)
    o = jnp.einsum('bnhqk,bnkhd->bnqhd', (p / l).astype(v.dtype), vb)
    lse = (m + jnp.log(l))[..., 0]
    o = o.reshape(Bt, nb * blk, H, D)[:, :L]
    lse = lse.transpose(0, 1, 3, 2).reshape(Bt, nb * blk, H)[:, :L]
    return o, lse


def dilated_sliding_window_attention(q, k, v):
    B, S, _, D = q.shape
    Hg = DIL_HEADS_PER_GROUP
    outs, lses = [], []
    for gi, (window, dil) in enumerate(DIL_GROUPS):
        lo, hi = gi * Hg, (gi + 1) * Hg

        def sub(t):
            return (t[:, :, lo:hi].reshape(B, S // dil, dil, Hg, D)
                    .transpose(0, 2, 1, 3, 4).reshape(B * dil, S // dil, Hg, D))

        o, lse = banded_attention(sub(q), sub(k), sub(v), window // dil)
        outs.append(o.reshape(B, dil, S // dil, Hg, D).transpose(0, 2, 1, 3, 4).reshape(B, S, Hg, D))
        lses.append(lse.reshape(B, dil, S // dil, Hg).transpose(0, 2, 1, 3).reshape(B, S, Hg))
    wts = jax.nn.softmax(jnp.stack(lses, 0), axis=0)
    return jnp.einsum('gbsh,gbshd->bshd', wts.astype(q.dtype), jnp.stack(outs, 0))


def hybrid_mixer(x, positions, w_in, conv_w, a_log, dt_bias, dn_norm_g, w_br_a, w_br_b, w_mix_out):
    B, S, _ = x.shape
    f32 = jnp.float32
    splits = np.cumsum(IN_SIZES)[:-1].tolist()
    a_qkv, a_z, a_decay, a_beta, b_q, b_k, b_v, gate_a, gate_b = jnp.split(x @ w_in, splits, axis=-1)

    qkv = jax.nn.silu(causal_depthwise_conv(a_qkv, conv_w)).astype(f32)
    dq, dk, dv = jnp.split(qkv, [DN_QK, 2 * DN_QK], axis=-1)
    hd = (B, S, DN_HEADS, DN_HEAD_DIM)
    dq = l2_normalize(dq.reshape(hd)) * (DN_HEAD_DIM ** -0.5)
    dk = l2_normalize(dk.reshape(hd))
    dv = dv.reshape(hd)
    g = -jnp.exp(a_log.astype(f32)) * jax.nn.softplus(a_decay.astype(f32) + dt_bias.astype(f32))
    beta = jax.nn.sigmoid(a_beta.astype(f32))
    o_a = gated_delta_rule_chunked(dq, dk, dv, g, beta)
    o_a = o_a * lax.rsqrt(jnp.mean(jnp.square(o_a), -1, keepdims=True) + RMS_EPS) * dn_norm_g.astype(f32)
    o_a = (o_a * jax.nn.silu(a_z.astype(f32)).reshape(hd)).reshape(B, S, DN_V).astype(x.dtype)
    y_a = o_a @ w_br_a

    hb = (B, S, DIL_HEADS, DIL_HEAD_DIM)
    bq = rope(b_q.reshape(hb), positions)
    bk = rope(b_k.reshape(hb), positions)
    o_b = dilated_sliding_window_attention(bq, bk, b_v.reshape(hb)).reshape(B, S, DIL_OUT)
    y_b = o_b @ w_br_b

    h = jax.nn.sigmoid(gate_a) * y_a + jax.nn.sigmoid(gate_b) * y_b
    return h @ w_mix_out


def memory_cross_attention(x, mem, w_q, w_kv, w_o):
    B, S, _ = x.shape
    M = mem.shape[1]
    q = (x @ w_q).reshape(B, S, MEM_HEADS, MEM_HEAD_DIM)
    k, v = jnp.split(mem @ w_kv, 2, axis=-1)
    k = k.reshape(B, M, MEM_HEADS, MEM_HEAD_DIM)
    v = v.reshape(B, M, MEM_HEADS, MEM_HEAD_DIM)
    s = jnp.einsum('bshd,bmhd->bhsm', q, k).astype(jnp.float32) * (MEM_HEAD_DIM ** -0.5)
    p = jax.nn.softmax(s, axis=-1).astype(x.dtype)
    o = jnp.einsum('bhsm,bmhd->bshd', p, v).reshape(B, S, D_MODEL)
    return o @ w_o


def swiglu_ffn(x, w13, w2):
    a, b = jnp.split(x @ w13, 2, axis=-1)
    return (jax.nn.silu(a) * b) @ w2


def setup_inputs(seed: int = 0) -> dict:
    key = jax.random.key(seed)
    ks = jax.random.split(key, 24)
    L = DEPTH
    f32 = jnp.float32

    def dense(k, fan_in, fan_out, scale=1.0):
        return jax.random.normal(k, (L, fan_in, fan_out), f32) * (scale * fan_in ** -0.5)

    def gain(k, n):
        return 1.0 + 0.02 * jax.random.normal(k, (L, n), f32)

    def bias(k, n):
        return 0.02 * jax.random.normal(k, (L, n), f32)

    x = jax.random.normal(ks[0], (BATCH, SEQ, D_MODEL), f32)
    mem = jax.random.normal(ks[1], (BATCH, MEM_TOKENS, D_MODEL), f32)
    offset = jax.random.randint(ks[2], (BATCH, 1), 0, 4096, jnp.int32)
    positions = offset + jnp.arange(SEQ, dtype=jnp.int32)[None, :]
    w_in = dense(ks[3], D_MODEL, IN_WIDTH)
    conv_w = 0.5 * jax.random.normal(ks[4], (L, DN_CONV, 2 * DN_QK + DN_V), f32)
    a_log = jnp.log(jax.random.uniform(ks[5], (L, DN_HEADS), f32, 1.0, 16.0))
    dt = jnp.exp(jax.random.uniform(ks[6], (L, DN_HEADS), f32, math.log(1e-3), math.log(1e-1)))
    dt_bias = dt + jnp.log(-jnp.expm1(-dt))
    dn_norm_g = gain(ks[7], DN_HEAD_DIM)
    w_br_a = dense(ks[8], DN_V, D_MODEL)
    w_br_b = dense(ks[9], DIL_OUT, D_MODEL)
    w_mix_out = dense(ks[10], D_MODEL, D_MODEL, DEEPNORM_BETA)
    ln1_g = gain(ks[11], D_MODEL)
    ln1_b = bias(ks[12], D_MODEL)
    w_xq = dense(ks[13], D_MODEL, D_MODEL)
    w_xkv = dense(ks[14], D_MODEL, 2 * D_MODEL)
    w_xo = dense(ks[15], D_MODEL, D_MODEL, DEEPNORM_BETA)
    ln2_g = gain(ks[16], D_MODEL)
    ln2_b = bias(ks[17], D_MODEL)
    w_ffn13 = dense(ks[18], D_MODEL, 2 * D_FF)
    w_ffn2 = dense(ks[19], D_FF, D_MODEL, DEEPNORM_BETA)
    ln3_g = gain(ks[20], D_MODEL)
    ln3_b = bias(ks[21], D_MODEL)
    return {'x': x, 'mem': mem, 'positions': positions, 'w_in': w_in, 'conv_w': conv_w,
            'a_log': a_log, 'dt_bias': dt_bias, 'dn_norm_g': dn_norm_g, 'w_br_a': w_br_a,
            'w_br_b': w_br_b, 'w_mix_out': w_mix_out, 'ln1_g': ln1_g, 'ln1_b': ln1_b,
            'w_xq': w_xq, 'w_xkv': w_xkv, 'w_xo': w_xo, 'ln2_g': ln2_g, 'ln2_b': ln2_b,
            'w_ffn13': w_ffn13, 'w_ffn2': w_ffn2, 'ln3_g': ln3_g, 'ln3_b': ln3_b}


def reference(x, mem, positions, w_in, conv_w, a_log, dt_bias, dn_norm_g, w_br_a, w_br_b,
              w_mix_out, ln1_g, ln1_b, w_xq, w_xkv, w_xo, ln2_g, ln2_b, w_ffn13, w_ffn2,
              ln3_g, ln3_b):
    for l in range(DEPTH):
        mix = hybrid_mixer(x, positions, w_in[l], conv_w[l], a_log[l], dt_bias[l], dn_norm_g[l],
                           w_br_a[l], w_br_b[l], w_mix_out[l])
        x = layer_norm(DEEPNORM_ALPHA * x + mix, ln1_g[l], ln1_b[l])
        cross = memory_cross_attention(x, mem, w_xq[l], w_xkv[l], w_xo[l])
        x = layer_norm(DEEPNORM_ALPHA * x + cross, ln2_g[l], ln2_b[l])
        ffn = swiglu_ffn(x, w_ffn13[l], w_ffn2[l])
        x = layer_norm(DEEPNORM_ALPHA * x + ffn, ln3_g[l], ln3_b[l])
    return x
```

```python
import functools
import math

import jax
import jax.numpy as jnp
from jax import lax
from jax.experimental import pallas as pl
from jax.experimental.pallas import tpu as pltpu

F32 = jnp.float32
BF16 = jnp.bfloat16

LANES = 128
D_MODEL = 1024
DN_HEADS = 8
HEAD_DIM = 128
DN_W = DN_HEADS * HEAD_DIM
DN_CONV = 4
DN_BLOCK = 128
DN_HPG = 2
DIL_GROUPS = ((128, 1), (512, 4), (2048, 16))
DIL_HPG = 4
DIL_W = len(DIL_GROUPS) * DIL_HPG * HEAD_DIM
DIL_OUT = DIL_HPG * HEAD_DIM
DIL_BLOCK = 128
ROPE_THETA = 10000.0
MEM_HEADS = 4
MEM_HEAD_DIM = D_MODEL // MEM_HEADS
D_FF = 2816
FF_CHUNK = 256
ALPHA = 2.0 ** 0.25
LN_EPS = 1e-5
RMS_EPS = 1e-6

COL_QKV = 0
COL_Z = 3 * DN_W
COL_GA = COL_Z + DN_W
COL_GB = COL_GA + D_MODEL
COL_BQ = COL_GB + D_MODEL
COL_BK = COL_BQ + DIL_W
COL_BV = COL_BK + DIL_W
COL_GATES = COL_BV + DIL_W
PROJ_TN = 1024
NP = 11264

VMEM_LIMIT = 52 * 1024 * 1024


def _cparams(n_axes):
    return pltpu.CompilerParams(dimension_semantics=("arbitrary",) * n_axes,
                                vmem_limit_bytes=VMEM_LIMIT)


def _dot(a, b):
    return jnp.dot(a, b, preferred_element_type=F32)


def _dot_nt(a, b):
    return lax.dot_general(a, b, (((1,), (1,)), ((), ())), preferred_element_type=F32)


def _sigmoid(x):
    return 1.0 / (1.0 + jnp.exp(-x))


def _layer_norm(y, g, b):
    mu = jnp.mean(y, axis=-1, keepdims=True)
    yc = y - mu
    var = jnp.mean(yc * yc, axis=-1, keepdims=True)
    return yc * lax.rsqrt(var + LN_EPS) * g + b


def _proj_kernel(x_ref, w_ref, o_ref):
    o_ref[...] = _dot(x_ref[...], w_ref[...])


def _in_proj(xb, wb, tm):
    T = xb.shape[0]
    return pl.pallas_call(
        _proj_kernel,
        grid=(T // tm, NP // PROJ_TN),
        in_specs=[pl.BlockSpec((tm, D_MODEL), lambda i, j: (i, 0)),
                  pl.BlockSpec((D_MODEL, PROJ_TN), lambda i, j: (0, j))],
        out_specs=pl.BlockSpec((tm, PROJ_TN), lambda i, j: (i, j)),
        out_shape=jax.ShapeDtypeStruct((T, NP), F32),
        compiler_params=_cparams(2),
        name="in_proj",
    )(xb, wb)


def _conv_silu(x, w):
    row = lax.broadcasted_iota(jnp.int32, x.shape, 0)
    out = x * w[DN_CONV - 1:DN_CONV, :]
    for j in range(DN_CONV - 1):
        sh = DN_CONV - 1 - j
        xs = jnp.where(row >= sh, pltpu.roll(x, sh, 0), 0.0)
        out = out + xs * w[j:j + 1, :]
    return out * _sigmoid(out)


def _l2norm(x):
    return x * lax.rsqrt(jnp.sum(x * x, axis=-1, keepdims=True) + RMS_EPS)


def _deltanet_kernel(q_ref, k_ref, v_ref, z_ref, gates_ref, cwq_ref, cwk_ref, cwv_ref,
                     alog_ref, dtb_ref, ng_ref, o_ref,
                     q_s, k_s, kb_s, vb_s, g_s, wq_s, qk_s, kdt_s, u_s, gt_s):
    S = q_ref.shape[0]
    nblk = S // DN_BLOCK
    hg = pl.program_id(1)

    G = gates_ref[...]
    zz = G + dtb_ref[...]
    softplus = jnp.maximum(zz, 0.0) + jnp.log1p(jnp.exp(-jnp.abs(zz)))
    gam = -jnp.exp(alog_ref[...]) * softplus
    rowc = lax.broadcasted_iota(jnp.int32, G.shape, 0) & (DN_BLOCK - 1)
    sh = 1
    while sh < DN_BLOCK:
        gam = gam + jnp.where(rowc >= sh, pltpu.roll(gam, sh, 0), 0.0)
        sh *= 2
    beta_all = _sigmoid(G)
    lane = lax.broadcasted_iota(jnp.int32, G.shape, 1)

    qc = _conv_silu(q_ref[...], cwq_ref[...])
    kc = _conv_silu(k_ref[...], cwk_ref[...])
    vc = _conv_silu(v_ref[...], cwv_ref[...])
    for i in range(DN_HPG):
        h = hg * DN_HPG + i
        cols = slice(i * HEAD_DIM, (i + 1) * HEAD_DIM)
        gam_col = jnp.sum(jnp.where(lane == h, gam, 0.0), axis=1, keepdims=True)
        beta_col = jnp.sum(jnp.where(lane == h + DN_HEADS, beta_all, 0.0), axis=1, keepdims=True)
        kn = _l2norm(kc[:, cols])
        q_s[i] = _l2norm(qc[:, cols]) * (HEAD_DIM ** -0.5)
        k_s[i] = kn
        kb_s[i] = kn * beta_col
        vb_s[i] = vc[:, cols] * beta_col
        g_s[i] = jnp.broadcast_to(gam_col, (S, HEAD_DIM))

    ri = lax.broadcasted_iota(jnp.int32, (DN_BLOCK, DN_BLOCK), 0)
    ci = lax.broadcasted_iota(jnp.int32, (DN_BLOCK, DN_BLOCK), 1)
    eye = (ri == ci).astype(F32)
    n_double = int(math.log2(DN_BLOCK)) - 1

    def prep(c, carry):
        rows = pl.ds(pl.multiple_of(c * DN_BLOCK, DN_BLOCK), DN_BLOCK)
        for i in range(DN_HPG):
            q = q_s[i, rows, :]
            k = k_s[i, rows, :]
            kb = kb_s[i, rows, :]
            vb = vb_s[i, rows, :]
            gb = g_s[i, rows, :]
            diff = gb - gb.T
            dec = jnp.exp(jnp.where(ri >= ci, diff, -jnp.inf))
            a = _dot_nt(jnp.concatenate([q, kb], axis=0).astype(BF16), k.astype(BF16))
            qk = a[:DN_BLOCK] * dec
            n = -a[DN_BLOCK:] * jnp.where(ri > ci, dec, 0.0)
            t = eye + n
            p = _dot(n.astype(BF16), n.astype(BF16))
            for it in range(n_double):
                pb = p.astype(BF16)
                if it < n_double - 1:
                    r = _dot(jnp.concatenate([t, p], axis=0).astype(BF16), pb)
                    t = t + r[:DN_BLOCK]
                    p = r[DN_BLOCK:]
                else:
                    t = t + _dot(t.astype(BF16), pb)
            eg = jnp.exp(gb)
            uw = _dot(t.astype(BF16), jnp.concatenate([vb, kb * eg], axis=1).astype(BF16))
            gl = jnp.broadcast_to(gb[DN_BLOCK - 1:DN_BLOCK, :], (DN_BLOCK, HEAD_DIM))
            kd = k * jnp.exp(gl - gb)
            wq_s[i, c] = jnp.concatenate([uw[:, HEAD_DIM:], q * eg], axis=0).astype(BF16)
            qk_s[i, c] = qk.astype(BF16)
            kdt_s[i, c] = kd.T.astype(BF16)
            u_s[i, c] = uw[:, :HEAD_DIM]
            gt_s[i, c] = jnp.exp(gl)
        return carry

    lax.fori_loop(0, nblk, prep, 0)

    ng = ng_ref[...]

    def scan(c, states):
        rows = pl.ds(pl.multiple_of(c * DN_BLOCK, DN_BLOCK), DN_BLOCK)
        new_states = []
        for i in range(DN_HPG):
            cols = slice(i * HEAD_DIM, (i + 1) * HEAD_DIM)
            st = states[i]
            r = _dot(wq_s[i, c], st.astype(BF16))
            vn = (u_s[i, c] - r[:DN_BLOCK]).astype(BF16)
            o = r[DN_BLOCK:] + _dot(qk_s[i, c], vn)
            new_states.append(gt_s[i, c] * st + _dot(kdt_s[i, c], vn))
            o = o * lax.rsqrt(jnp.mean(o * o, axis=-1, keepdims=True) + RMS_EPS) * ng
            z = z_ref[rows, cols]
            o_ref[rows, cols] = (o * (z * _sigmoid(z))).astype(o_ref.dtype)
        return tuple(new_states)

    init = tuple(jnp.zeros((HEAD_DIM, HEAD_DIM), F32) for _ in range(DN_HPG))
    lax.fori_loop(0, nblk, scan, init)


def _deltanet(P3, conv_w, alog_row, dtb_row, ng_row):
    B, S, _ = P3.shape
    W = DN_HPG * HEAD_DIM
    ngrp = DN_HEADS // DN_HPG
    nblk = S // DN_BLOCK

    def col_spec(base):
        return pl.BlockSpec((None, S, W), lambda b, g: (b, 0, base // W + g))

    def cw_spec(base):
        return pl.BlockSpec((DN_CONV, W), lambda b, g: (0, base // W + g))

    row_spec = pl.BlockSpec((1, LANES), lambda b, g: (0, 0))
    head_buf = lambda dt: pltpu.VMEM((DN_HPG, S, HEAD_DIM), dt)
    blk_buf = lambda rows, dt: pltpu.VMEM((DN_HPG, nblk, rows, HEAD_DIM), dt)
    return pl.pallas_call(
        _deltanet_kernel,
        grid=(B, ngrp),
        in_specs=[col_spec(COL_QKV), col_spec(COL_QKV + DN_W), col_spec(COL_QKV + 2 * DN_W), col_spec(COL_Z),
                  pl.BlockSpec((None, S, LANES), lambda b, g: (b, 0, COL_GATES // LANES)),
                  cw_spec(0), cw_spec(DN_W), cw_spec(2 * DN_W),
                  row_spec, row_spec, row_spec],
        out_specs=pl.BlockSpec((None, S, W), lambda b, g: (b, 0, g)),
        out_shape=jax.ShapeDtypeStruct((B, S, DN_W), BF16),
        scratch_shapes=[head_buf(F32), head_buf(F32), head_buf(F32), head_buf(F32), head_buf(F32),
                        blk_buf(2 * DN_BLOCK, BF16), blk_buf(DN_BLOCK, BF16), blk_buf(DN_BLOCK, BF16),
                        blk_buf(DN_BLOCK, F32), blk_buf(DN_BLOCK, F32)],
        compiler_params=_cparams(2),
        name="deltanet",
    )(P3, P3, P3, P3, P3, conv_w, conv_w, conv_w, alog_row, dtb_row, ng_row)


def _dilated_kernel(q0_ref, q1_ref, q2_ref, k0_ref, k1_ref, k2_ref, v0_ref, v1_ref, v2_ref,
                    pos_ref, invf_ref, o_ref, cos_s, sin_s, qr_s, kr_s, og_s, lg_s):
    S = q0_ref.shape[0]
    blk = DIL_BLOCK
    scale = HEAD_DIM ** -0.5

    @pl.when(pl.program_id(1) == 0)
    def _():
        ang = pos_ref[...].astype(F32) * invf_ref[...]
        lane = lax.broadcasted_iota(jnp.int32, ang.shape, 1)
        cos_s[...] = jnp.cos(ang)
        sin_s[...] = jnp.where(lane < HEAD_DIM // 2, -jnp.sin(ang), jnp.sin(ang))

    def rope(x):
        return x * cos_s[...] + pltpu.roll(x, HEAD_DIM // 2, 1) * sin_s[...]

    ri = lax.broadcasted_iota(jnp.int32, (blk, blk), 0)
    ci = lax.broadcasted_iota(jnp.int32, (blk, blk), 1)
    neg = -jnp.inf

    def rows_of(start, dil):
        return pl.ds(start, blk) if dil == 1 else pl.ds(start, blk, stride=dil)

    def block(g, v_ref, start, dil, has_prev):
        rows = rows_of(start, dil)
        q = qr_s[rows, :].astype(BF16)
        s = jnp.where(ri >= ci, _dot_nt(q, kr_s[rows, :].astype(BF16)) * scale, neg)
        m = jnp.max(s, axis=-1, keepdims=True)
        if has_prev:
            prows = rows_of(start - blk * dil, dil)
            sp = jnp.where(ci >= ri, _dot_nt(q, kr_s[prows, :].astype(BF16)) * scale, neg)
            m = jnp.maximum(m, jnp.max(sp, axis=-1, keepdims=True))
            pp = jnp.exp(sp - m)
        p = jnp.exp(s - m)
        l = jnp.sum(p, axis=-1, keepdims=True)
        if has_prev:
            l = l + jnp.sum(pp, axis=-1, keepdims=True)
        inv = 1.0 / l
        o = _dot((p * inv).astype(BF16), v_ref[rows, :].astype(BF16))
        if has_prev:
            o = o + _dot((pp * inv).astype(BF16), v_ref[prows, :].astype(BF16))
        og_s[g, rows, :] = o
        lg_s[g, rows, :] = jnp.broadcast_to(m + jnp.log(l), (blk, HEAD_DIM))

    q_refs = (q0_ref, q1_ref, q2_ref)
    k_refs = (k0_ref, k1_ref, k2_ref)
    v_refs = (v0_ref, v1_ref, v2_ref)
    for g, (window, dil) in enumerate(DIL_GROUPS):
        assert window // dil == blk
        qr_s[...] = rope(q_refs[g][...])
        kr_s[...] = rope(k_refs[g][...])
        nb = S // dil // blk
        v_ref = v_refs[g]

        def first(r, carry, g=g, v_ref=v_ref, dil=dil):
            block(g, v_ref, r, dil, False)
            return carry

        lax.fori_loop(0, dil, first, 0)
        if nb > 1:
            def later(idx, carry, g=g, v_ref=v_ref, dil=dil):
                n = idx // dil + 1
                r = idx % dil
                block(g, v_ref, n * (blk * dil) + r, dil, True)
                return carry

            lax.fori_loop(0, (nb - 1) * dil, later, 0)

    l0, l1, l2 = lg_s[0], lg_s[1], lg_s[2]
    m = jnp.maximum(jnp.maximum(l0, l1), l2)
    e0, e1, e2 = jnp.exp(l0 - m), jnp.exp(l1 - m), jnp.exp(l2 - m)
    inv = 1.0 / (e0 + e1 + e2)
    o_ref[...] = ((e0 * inv) * og_s[0] + (e1 * inv) * og_s[1] + (e2 * inv) * og_s[2]).astype(o_ref.dtype)


def _dilated(P3, pos3, invf_row):
    B, S, _ = P3.shape

    def head_spec(base, g):
        return pl.BlockSpec((None, S, HEAD_DIM), lambda b, h: (b, 0, base // HEAD_DIM + g * DIL_HPG + h))

    ngr = len(DIL_GROUPS)
    in_specs = ([head_spec(COL_BQ, g) for g in range(ngr)] + [head_spec(COL_BK, g) for g in range(ngr)]
                + [head_spec(COL_BV, g) for g in range(ngr)]
                + [pl.BlockSpec((None, S, 1), lambda b, h: (b, 0, 0)),
                   pl.BlockSpec((1, HEAD_DIM), lambda b, h: (0, 0))])
    big = pltpu.VMEM((S, HEAD_DIM), F32)
    return pl.pallas_call(
        _dilated_kernel,
        grid=(B, DIL_HPG),
        in_specs=in_specs,
        out_specs=pl.BlockSpec((None, S, HEAD_DIM), lambda b, h: (b, 0, h)),
        out_shape=jax.ShapeDtypeStruct((B, S, DIL_OUT), BF16),
        scratch_shapes=[big, big, big, big,
                        pltpu.VMEM((ngr, S, HEAD_DIM), F32), pltpu.VMEM((ngr, S, HEAD_DIM), F32)],
        compiler_params=_cparams(2),
        name="dilated",
    )(*([P3] * 9), pos3, invf_row)


def _mix_kernel(oa_ref, ob_ref, ga_ref, gb_ref, x_ref, wa_ref, wb_ref, wm_ref, g_ref, b_ref, o_ref):
    ya = _dot(oa_ref[...], wa_ref[...])
    yb = _dot(ob_ref[...], wb_ref[...])
    h = _sigmoid(ga_ref[...]) * ya + _sigmoid(gb_ref[...]) * yb
    mix = _dot(h.astype(BF16), wm_ref[...])
    o_ref[...] = _layer_norm(ALPHA * x_ref[...] + mix, g_ref[...], b_ref[...])


def _const_spec(shape):
    return pl.BlockSpec(shape, lambda *_: (0,) * len(shape), pipeline_mode=pl.Buffered(1))


def _mix(oa, ob, P, x2d, wa, wb, wm, g, b, tm):
    T = x2d.shape[0]
    row = lambda w, j=0: pl.BlockSpec((tm, w), lambda i: (i, j))
    return pl.pallas_call(
        _mix_kernel,
        grid=(T // tm,),
        in_specs=[row(DN_W), row(DIL_OUT), row(D_MODEL, COL_GA // D_MODEL), row(D_MODEL, COL_GB // D_MODEL),
                  row(D_MODEL), _const_spec(wa.shape), _const_spec(wb.shape), _const_spec(wm.shape),
                  _const_spec(g.shape), _const_spec(b.shape)],
        out_specs=row(D_MODEL),
        out_shape=jax.ShapeDtypeStruct((T, D_MODEL), F32),
        compiler_params=_cparams(1),
        name="mix",
    )(oa, ob, P, P, x2d, wa, wb, wm, g, b)


def _memkv_kernel(m_ref, w_ref, o_ref):
    o_ref[...] = _dot(m_ref[...].astype(BF16), w_ref[...]).astype(o_ref.dtype)


def _mem_kv(mem2d, wkv, tm):
    R = mem2d.shape[0]
    return pl.pallas_call(
        _memkv_kernel,
        grid=(R // tm,),
        in_specs=[pl.BlockSpec((tm, D_MODEL), lambda i: (i, 0)), _const_spec(wkv.shape)],
        out_specs=pl.BlockSpec((tm, 2 * D_MODEL), lambda i: (i, 0)),
        out_shape=jax.ShapeDtypeStruct((R, 2 * D_MODEL), BF16),
        compiler_params=_cparams(1),
        name="mem_kv",
    )(mem2d, wkv)


def _xattn_kernel(x_ref, kv_ref, wq_ref, wo_ref, g_ref, b_ref, o_ref):
    x = x_ref[...]
    q = _dot(x.astype(BF16), wq_ref[...])
    scale = MEM_HEAD_DIM ** -0.5
    outs = []
    for h in range(MEM_HEADS):
        cols = slice(h * MEM_HEAD_DIM, (h + 1) * MEM_HEAD_DIM)
        vcols = slice(D_MODEL + h * MEM_HEAD_DIM, D_MODEL + (h + 1) * MEM_HEAD_DIM)
        s = _dot_nt(q[:, cols].astype(BF16), kv_ref[:, cols]) * scale
        p = jnp.exp(s - jnp.max(s, axis=-1, keepdims=True))
        p = p * (1.0 / jnp.sum(p, axis=-1, keepdims=True))
        outs.append(_dot(p.astype(BF16), kv_ref[:, vcols]).astype(BF16))
    cross = _dot(jnp.concatenate(outs, axis=1), wo_ref[...])
    o_ref[...] = _layer_norm(ALPHA * x + cross, g_ref[...], b_ref[...])


def _xattn(x3, kv3, wq, wo, g, b, tm):
    B, S, _ = x3.shape
    M = kv3.shape[1]
    return pl.pallas_call(
        _xattn_kernel,
        grid=(B, S // tm),
        in_specs=[pl.BlockSpec((None, tm, D_MODEL), lambda bi, i: (bi, i, 0)),
                  pl.BlockSpec((None, M, 2 * D_MODEL), lambda bi, i: (bi, 0, 0)),
                  _const_spec(wq.shape), _const_spec(wo.shape), _const_spec(g.shape), _const_spec(b.shape)],
        out_specs=pl.BlockSpec((None, tm, D_MODEL), lambda bi, i: (bi, i, 0)),
        out_shape=jax.ShapeDtypeStruct((B, S, D_MODEL), F32),
        compiler_params=_cparams(2),
        name="xattn",
    )(x3, kv3, wq, wo, g, b)


def _ffn_kernel(x_ref, w13_ref, w2_ref, g_ref, b_ref, o_ref):
    x = x_ref[...]
    xb = x.astype(BF16)
    acc = jnp.zeros(x.shape, F32)
    for j in range(D_FF // FF_CHUNK):
        ca = slice(j * FF_CHUNK, (j + 1) * FF_CHUNK)
        cb = slice(D_FF + j * FF_CHUNK, D_FF + (j + 1) * FF_CHUNK)
        a = _dot(xb, w13_ref[:, ca])
        h = (a * _sigmoid(a)) * _dot(xb, w13_ref[:, cb])
        acc = acc + _dot(h.astype(BF16), w2_ref[ca, :])
    o_ref[...] = _layer_norm(ALPHA * x + acc, g_ref[...], b_ref[...])


def _ffn(x2d, w13, w2, g, b, tm):
    T = x2d.shape[0]
    return pl.pallas_call(
        _ffn_kernel,
        grid=(T // tm,),
        in_specs=[pl.BlockSpec((tm, D_MODEL), lambda i: (i, 0)),
                  _const_spec(w13.shape), _const_spec(w2.shape), _const_spec(g.shape), _const_spec(b.shape)],
        out_specs=pl.BlockSpec((tm, D_MODEL), lambda i: (i, 0)),
        out_shape=jax.ShapeDtypeStruct((T, D_MODEL), F32),
        compiler_params=_cparams(1),
        name="ffn",
    )(x2d, w13, w2, g, b)


def _relayout_w_in(w):
    sizes = (3 * DN_W, DN_W, DN_HEADS, DN_HEADS, DIL_W, DIL_W, DIL_W, D_MODEL, D_MODEL)
    offs = [0]
    for s in sizes:
        offs.append(offs[-1] + s)
    part = lambda i: w[:, offs[i]:offs[i + 1]]
    used = COL_GATES + 2 * DN_HEADS
    pad = jnp.zeros((w.shape[0], NP - used), w.dtype)
    return jnp.concatenate([part(0), part(1), part(7), part(8), part(4), part(5), part(6),
                            part(2), part(3), pad], axis=1)


def _pad_row(v, fill=0.0):
    return jnp.concatenate([v.astype(F32), jnp.full((LANES - v.shape[0],), fill, F32)])[None, :]


def kernel(x, mem, positions, w_in, conv_w, a_log, dt_bias, dn_norm_g, w_br_a, w_br_b, w_mix_out,
           ln1_g, ln1_b, w_xq, w_xkv, w_xo, ln2_g, ln2_b, w_ffn13, w_ffn2, ln3_g, ln3_b):
    B, S, D = x.shape
    T = B * S
    M = mem.shape[1]
    assert D == D_MODEL and w_in.shape[0] == 1 and S % (DIL_BLOCK * DIL_GROUPS[-1][1]) == 0
    x2d = x.reshape(T, D)
    bf = lambda w: w[0].astype(BF16)
    row = lambda v: v[0][None, :].astype(F32)

    P = _in_proj(x2d.astype(BF16), _relayout_w_in(w_in[0]).astype(BF16), tm=S)
    P3 = P.reshape(B, S, NP)

    o_a = _deltanet(P3, conv_w[0], _pad_row(a_log[0]), _pad_row(dt_bias[0]), row(dn_norm_g))

    inv_freq = ROPE_THETA ** (-jnp.arange(0, HEAD_DIM, 2, dtype=F32) / HEAD_DIM)
    invf_row = jnp.concatenate([inv_freq, inv_freq])[None, :]
    o_b = _dilated(P3, positions.reshape(B, S, 1), invf_row)

    x1 = _mix(o_a.reshape(T, DN_W), o_b.reshape(T, DIL_OUT), P, x2d, bf(w_br_a), bf(w_br_b), bf(w_mix_out),
              row(ln1_g), row(ln1_b), tm=512)

    kv = _mem_kv(mem.reshape(B * M, D), bf(w_xkv), tm=512)
    x2 = _xattn(x1.reshape(B, S, D), kv.reshape(B, M, 2 * D), bf(w_xq), bf(w_xo), row(ln2_g), row(ln2_b), tm=512)

    out = _ffn(x2.reshape(T, D), bf(w_ffn13), bf(w_ffn2), row(ln3_g), row(ln3_b), tm=512)
    return out.reshape(B, S, D)
```

```python
import functools
import math

import jax
import jax.numpy as jnp
from jax import lax
from jax.experimental import pallas as pl
from jax.experimental.pallas import tpu as pltpu

F32 = jnp.float32
BF16 = jnp.bfloat16

LANES = 128
D_MODEL = 1024
DN_HEADS = 8
HEAD_DIM = 128
DN_W = DN_HEADS * HEAD_DIM
DN_CONV = 4
DN_BLOCK = 128
DN_HPG = 2
INV_BASE = 16
DN_STAGE_BLOCKS = 4
DIL_GROUPS = ((128, 1), (512, 4), (2048, 16))
DIL_HPG = 4
DIL_W = len(DIL_GROUPS) * DIL_HPG * HEAD_DIM
DIL_OUT = DIL_HPG * HEAD_DIM
DIL_BLOCK = 128
DIL_UNROLL = 4
ROPE_THETA = 10000.0
MEM_HEADS = 4
MEM_HEAD_DIM = D_MODEL // MEM_HEADS
D_FF = 2816
FF_CHUNK = 256
ALPHA = 2.0 ** 0.25
LN_EPS = 1e-5
RMS_EPS = 1e-6

COL_QKV = 0
COL_Z = 3 * DN_W
COL_GA = COL_Z + DN_W
COL_GB = COL_GA + D_MODEL
COL_BQ = COL_GB + D_MODEL
COL_BK = COL_BQ + DIL_W
COL_BV = COL_BK + DIL_W
COL_GATES = COL_BV + DIL_W
PROJ_TN = 1024
NP = 11264

VMEM_LIMIT = 52 * 1024 * 1024


def _cparams(n_axes):
    return pltpu.CompilerParams(dimension_semantics=("arbitrary",) * n_axes,
                                vmem_limit_bytes=VMEM_LIMIT)


def _dot(a, b):
    return jnp.dot(a, b, preferred_element_type=F32)


def _dot_nt(a, b):
    return lax.dot_general(a, b, (((1,), (1,)), ((), ())), preferred_element_type=F32)


def _sigmoid(x):
    return 1.0 / (1.0 + jnp.exp(-x))


def _silu(x):
    return x * _sigmoid(x)


def _layer_norm(y, g, b):
    mu = jnp.mean(y, axis=-1, keepdims=True)
    yc = y - mu
    var = jnp.mean(yc * yc, axis=-1, keepdims=True)
    return yc * lax.rsqrt(var + LN_EPS) * g + b


def _proj_kernel(x_ref, w_ref, o_ref):
    o_ref[...] = _dot(x_ref[...], w_ref[...])


def _in_proj(xb, wb, tm):
    T = xb.shape[0]
    return pl.pallas_call(
        _proj_kernel,
        grid=(T // tm, NP // PROJ_TN),
        in_specs=[pl.BlockSpec((tm, D_MODEL), lambda i, j: (i, 0)),
                  pl.BlockSpec((D_MODEL, PROJ_TN), lambda i, j: (0, j))],
        out_specs=pl.BlockSpec((tm, PROJ_TN), lambda i, j: (i, j)),
        out_shape=jax.ShapeDtypeStruct((T, NP), F32),
        compiler_params=_cparams(2),
        name="in_proj",
    )(xb, wb)


def _conv_silu_block(x_ref, w, base, first):
    halo = 8
    if first:
        cur = x_ref[pl.ds(0, DN_BLOCK), :]
        xw = jnp.concatenate([jnp.zeros((halo, cur.shape[1]), cur.dtype), cur], axis=0)
    else:
        xw = x_ref[pl.ds(pl.multiple_of(base - halo, halo), DN_BLOCK + halo), :]
    out = xw[halo:, :] * w[DN_CONV - 1:DN_CONV, :]
    for j in range(DN_CONV - 1):
        lo = halo - (DN_CONV - 1 - j)
        out = out + xw[lo:lo + DN_BLOCK, :] * w[j:j + 1, :]
    return _silu(out)


def _l2norm(x):
    return x * lax.rsqrt(jnp.sum(x * x, axis=-1, keepdims=True) + RMS_EPS)


def _deltanet_kernel(q_ref, k_ref, v_ref, z_ref, gates_ref, cwq_ref, cwk_ref, cwv_ref,
                     alog_ref, dtb_ref, ng_ref, o_ref,
                     gam_s, beta_s, n_s, t_s, p_s, qk_s, rhs_s, wq_s, kdt_s, u_s, gt_s):
    S = q_ref.shape[0]
    nblk = S // DN_BLOCK
    hg = pl.program_id(1)
    blk = DN_BLOCK

    @pl.when(hg == 0)
    def _():
        G = gates_ref[...]
        zz = G + dtb_ref[...]
        softplus = jnp.maximum(zz, 0.0) + jnp.log1p(jnp.exp(-jnp.abs(zz)))
        gam = -jnp.exp(alog_ref[...]) * softplus
        rowc = lax.broadcasted_iota(jnp.int32, G.shape, 0) & (blk - 1)
        sh = 1
        while sh < blk:
            gam = gam + jnp.where(rowc >= sh, pltpu.roll(gam, sh, 0), 0.0)
            sh *= 2
        gam_s[...] = gam
        beta_s[...] = _sigmoid(G)

    ri = lax.broadcasted_iota(jnp.int32, (blk, blk), 0)
    ci = lax.broadcasted_iota(jnp.int32, (blk, blk), 1)

    def same_block(b):
        s = int(math.log2(b))
        return (ri >> s) == (ci >> s)

    def setup(c, first):
        base = c * blk
        rows = pl.ds(pl.multiple_of(base, blk), blk)
        conv = lambda r, w: _conv_silu_block(r, w[...], base, first)
        qc, kc, vc = conv(q_ref, cwq_ref), conv(k_ref, cwk_ref), conv(v_ref, cwv_ref)
        gsl = gam_s[rows, :]
        bsl = beta_s[rows, :]
        for i in range(DN_HPG):
            h = hg * DN_HPG + i
            cols = slice(i * HEAD_DIM, (i + 1) * HEAD_DIM)
            gam_col = jnp.sum(jnp.where(ci == h, gsl, 0.0), axis=1, keepdims=True)
            beta_col = jnp.sum(jnp.where(ci == h + DN_HEADS, bsl, 0.0), axis=1, keepdims=True)
            gb = jnp.broadcast_to(gam_col, (blk, HEAD_DIM))
            kn = _l2norm(kc[:, cols])
            qn = _l2norm(qc[:, cols]) * (HEAD_DIM ** -0.5)
            kb = kn * beta_col
            dec = jnp.exp(jnp.where(ri >= ci, gb - gb.T, -jnp.inf))
            a = _dot_nt(jnp.concatenate([qn, kb], axis=0).astype(BF16), kn.astype(BF16))
            qk_s[i, c] = (a[:blk] * dec).astype(BF16)
            n_s[i, c] = -a[blk:] * jnp.where(ri > ci, dec, 0.0)
            eg = jnp.exp(gb)
            rhs_s[i, c] = jnp.concatenate([vc[:, cols] * beta_col, kb * eg], axis=1).astype(BF16)
            wq_s[i, c, blk:, :] = (qn * eg).astype(BF16)
            gl = jnp.broadcast_to(gb[blk - 1:blk, :], (blk, HEAD_DIM))
            kdt_s[i, c] = (kn * jnp.exp(gl - gb)).T.astype(BF16)
            gt_s[i, c] = jnp.exp(gl[:8, :])

    setup(0, True)

    def setup_rest(c, carry):
        setup(c, False)
        return carry

    lax.fori_loop(1, nblk, setup_rest, 0)

    def stage(issue, finish):
        def body(it, carry):
            probs = [(i, it * DN_STAGE_BLOCKS + cc) for cc in range(DN_STAGE_BLOCKS) for i in range(DN_HPG)]
            results = [issue(i, c) for i, c in probs]
            for (i, c), r in zip(probs, results):
                finish(i, c, r)
            return carry

        lax.fori_loop(0, nblk // DN_STAGE_BLOCKS, body, 0)

    def series_start(i, c):
        d = jnp.where(same_block(INV_BASE), n_s[i, c], 0.0)
        return d, _dot(d.astype(BF16), d.astype(BF16))

    def series_start_done(i, c, r):
        t_s[i, c] = (ri == ci).astype(F32) + r[0]
        p_s[i, c] = r[1].astype(BF16)

    stage(series_start, series_start_done)

    def series_both(i, c):
        pb = p_s[i, c]
        return _dot(jnp.concatenate([t_s[i, c].astype(BF16), pb], axis=0), pb)

    def series_both_done(i, c, r):
        t_s[i, c] = t_s[i, c] + r[:blk]
        p_s[i, c] = r[blk:].astype(BF16)

    def times_p(i, c):
        return _dot(t_s[i, c].astype(BF16), p_s[i, c])

    def add_to_t(i, c, r):
        t_s[i, c] = t_s[i, c] + r

    for _ in range(int(math.log2(INV_BASE)) - 2):
        stage(series_both, series_both_done)
    stage(times_p, add_to_t)

    b = INV_BASE
    while b < blk:
        off = same_block(2 * b) & jnp.logical_not(same_block(b))

        def off_times_t(i, c, off=off):
            return _dot(jnp.where(off, n_s[i, c], 0.0).astype(BF16), t_s[i, c].astype(BF16))

        def to_p(i, c, r):
            p_s[i, c] = r.astype(BF16)

        stage(off_times_t, to_p)
        stage(times_p, add_to_t)
        b *= 2

    def solve(i, c):
        return _dot(t_s[i, c].astype(BF16), rhs_s[i, c])

    def solve_done(i, c, r):
        u_s[i, c] = r[:, :HEAD_DIM]
        wq_s[i, c, :blk, :] = r[:, HEAD_DIM:].astype(BF16)

    stage(solve, solve_done)

    ng = ng_ref[...]

    def scan(c, states):
        rows = pl.ds(pl.multiple_of(c * blk, blk), blk)
        new_states = []
        for i in range(DN_HPG):
            cols = slice(i * HEAD_DIM, (i + 1) * HEAD_DIM)
            st = states[i]
            r = _dot(wq_s[i, c], st.astype(BF16))
            vn = (u_s[i, c] - r[:blk]).astype(BF16)
            o = r[blk:] + _dot(qk_s[i, c], vn)
            g_tot = jnp.broadcast_to(gt_s[i, c][0:1, :], (HEAD_DIM, HEAD_DIM))
            new_states.append(g_tot * st + _dot(kdt_s[i, c], vn))
            o = o * lax.rsqrt(jnp.mean(o * o, axis=-1, keepdims=True) + RMS_EPS) * ng
            o_ref[rows, cols] = (o * _silu(z_ref[rows, cols])).astype(o_ref.dtype)
        return tuple(new_states)

    init = tuple(jnp.zeros((HEAD_DIM, HEAD_DIM), F32) for _ in range(DN_HPG))
    lax.fori_loop(0, nblk, scan, init)


def _deltanet(P3, conv_w, alog_row, dtb_row, ng_row):
    B, S, _ = P3.shape
    W = DN_HPG * HEAD_DIM
    ngrp = DN_HEADS // DN_HPG
    nblk = S // DN_BLOCK
    assert nblk % DN_STAGE_BLOCKS == 0

    def col_spec(base):
        return pl.BlockSpec((None, S, W), lambda b, g: (b, 0, base // W + g))

    def cw_spec(base):
        return pl.BlockSpec((DN_CONV, W), lambda b, g: (0, base // W + g))

    row_spec = pl.BlockSpec((1, LANES), lambda b, g: (0, 0))
    blk_buf = lambda rows, cols, dt: pltpu.VMEM((DN_HPG, nblk, rows, cols), dt)
    return pl.pallas_call(
        _deltanet_kernel,
        grid=(B, ngrp),
        in_specs=[col_spec(COL_QKV), col_spec(COL_QKV + DN_W), col_spec(COL_QKV + 2 * DN_W), col_spec(COL_Z),
                  pl.BlockSpec((None, S, LANES), lambda b, g: (b, 0, COL_GATES // LANES)),
                  cw_spec(0), cw_spec(DN_W), cw_spec(2 * DN_W),
                  row_spec, row_spec, row_spec],
        out_specs=pl.BlockSpec((None, S, W), lambda b, g: (b, 0, g)),
        out_shape=jax.ShapeDtypeStruct((B, S, DN_W), BF16),
        scratch_shapes=[pltpu.VMEM((S, LANES), F32), pltpu.VMEM((S, LANES), F32),
                        blk_buf(DN_BLOCK, DN_BLOCK, F32), blk_buf(DN_BLOCK, DN_BLOCK, F32),
                        blk_buf(DN_BLOCK, DN_BLOCK, BF16), blk_buf(DN_BLOCK, DN_BLOCK, BF16),
                        blk_buf(DN_BLOCK, 2 * HEAD_DIM, BF16),
                        blk_buf(2 * DN_BLOCK, HEAD_DIM, BF16),
                        blk_buf(HEAD_DIM, DN_BLOCK, BF16),
                        blk_buf(DN_BLOCK, HEAD_DIM, F32),
                        blk_buf(8, HEAD_DIM, F32)],
        compiler_params=_cparams(2),
        name="deltanet",
    )(P3, P3, P3, P3, P3, conv_w, conv_w, conv_w, alog_row, dtb_row, ng_row)


def _dilated_kernel(q0_ref, q1_ref, q2_ref, k0_ref, k1_ref, k2_ref, v0_ref, v1_ref, v2_ref,
                    pos_ref, invf_ref, o_ref, cos_s, sin_s, qr_s, kr_s, og_s, lg_s):
    S = q0_ref.shape[0]
    blk = DIL_BLOCK
    scale = HEAD_DIM ** -0.5

    @pl.when(pl.program_id(1) == 0)
    def _():
        ang = pos_ref[...].astype(F32) * invf_ref[...]
        lane = lax.broadcasted_iota(jnp.int32, ang.shape, 1)
        cos_s[...] = jnp.cos(ang)
        sin_s[...] = jnp.where(lane < HEAD_DIM // 2, -jnp.sin(ang), jnp.sin(ang))

    def rope(x):
        return x * cos_s[...] + pltpu.roll(x, HEAD_DIM // 2, 1) * sin_s[...]

    ri = lax.broadcasted_iota(jnp.int32, (blk, blk), 0)
    ci = lax.broadcasted_iota(jnp.int32, (blk, blk), 1)
    neg = -jnp.inf

    def rows_of(start, dil):
        return pl.ds(start, blk) if dil == 1 else pl.ds(start, blk, stride=dil)

    def block_group(g, v_ref, starts, dil, has_prev):
        rows = [rows_of(st, dil) for st in starts]
        q = [qr_s[r, :].astype(BF16) for r in rows]
        s = [_dot_nt(q[j], kr_s[r, :].astype(BF16)) for j, r in enumerate(rows)]
        if has_prev:
            prows = [rows_of(st - blk * dil, dil) for st in starts]
            sp = [_dot_nt(q[j], kr_s[r, :].astype(BF16)) for j, r in enumerate(prows)]
        ps, pps, lses = [], [], []
        for j in range(len(starts)):
            sj = jnp.where(ri >= ci, s[j] * scale, neg)
            m = jnp.max(sj, axis=-1, keepdims=True)
            if has_prev:
                spj = jnp.where(ci >= ri, sp[j] * scale, neg)
                m = jnp.maximum(m, jnp.max(spj, axis=-1, keepdims=True))
                pp = jnp.exp(spj - m)
            p = jnp.exp(sj - m)
            l = jnp.sum(p, axis=-1, keepdims=True)
            if has_prev:
                l = l + jnp.sum(pp, axis=-1, keepdims=True)
            inv = 1.0 / l
            ps.append((p * inv).astype(BF16))
            if has_prev:
                pps.append((pp * inv).astype(BF16))
            lses.append(m + jnp.log(l))
        for j, r in enumerate(rows):
            o = _dot(ps[j], v_ref[r, :].astype(BF16))
            if has_prev:
                o = o + _dot(pps[j], v_ref[prows[j], :].astype(BF16))
            og_s[g, r, :] = o
            lg_s[g, r, :] = jnp.broadcast_to(lses[j], (blk, HEAD_DIM))

    q_refs = (q0_ref, q1_ref, q2_ref)
    k_refs = (k0_ref, k1_ref, k2_ref)
    v_refs = (v0_ref, v1_ref, v2_ref)
    for g, (window, dil) in enumerate(DIL_GROUPS):
        assert window // dil == blk
        qr_s[...] = rope(q_refs[g][...])
        kr_s[...] = rope(k_refs[g][...])
        nb = S // dil // blk
        v_ref = v_refs[g]

        u0 = min(dil, DIL_UNROLL)

        def first(it, carry, g=g, v_ref=v_ref, dil=dil, u0=u0):
            block_group(g, v_ref, [it * u0 + j for j in range(u0)], dil, False)
            return carry

        lax.fori_loop(0, dil // u0, first, 0)
        n_later = (nb - 1) * dil
        if n_later:
            u1 = max(u for u in range(1, DIL_UNROLL + 1) if n_later % u == 0)

            def later(it, carry, g=g, v_ref=v_ref, dil=dil, u1=u1):
                starts = []
                for j in range(u1):
                    idx = it * u1 + j
                    starts.append((idx // dil + 1) * (blk * dil) + idx % dil)
                block_group(g, v_ref, starts, dil, True)
                return carry

            lax.fori_loop(0, n_later // u1, later, 0)

    l0, l1, l2 = lg_s[0], lg_s[1], lg_s[2]
    m = jnp.maximum(jnp.maximum(l0, l1), l2)
    e0, e1, e2 = jnp.exp(l0 - m), jnp.exp(l1 - m), jnp.exp(l2 - m)
    inv = 1.0 / (e0 + e1 + e2)
    o_ref[...] = ((e0 * inv) * og_s[0] + (e1 * inv) * og_s[1] + (e2 * inv) * og_s[2]).astype(o_ref.dtype)


def _dilated(P3, pos3, invf_row):
    B, S, _ = P3.shape

    def head_spec(base, g):
        return pl.BlockSpec((None, S, HEAD_DIM), lambda b, h: (b, 0, base // HEAD_DIM + g * DIL_HPG + h))

    ngr = len(DIL_GROUPS)
    in_specs = ([head_spec(COL_BQ, g) for g in range(ngr)] + [head_spec(COL_BK, g) for g in range(ngr)]
                + [head_spec(COL_BV, g) for g in range(ngr)]
                + [pl.BlockSpec((None, S, 1), lambda b, h: (b, 0, 0)),
                   pl.BlockSpec((1, HEAD_DIM), lambda b, h: (0, 0))])
    big = pltpu.VMEM((S, HEAD_DIM), F32)
    return pl.pallas_call(
        _dilated_kernel,
        grid=(B, DIL_HPG),
        in_specs=in_specs,
        out_specs=pl.BlockSpec((None, S, HEAD_DIM), lambda b, h: (b, 0, h)),
        out_shape=jax.ShapeDtypeStruct((B, S, DIL_OUT), BF16),
        scratch_shapes=[big, big, big, big,
                        pltpu.VMEM((ngr, S, HEAD_DIM), F32), pltpu.VMEM((ngr, S, HEAD_DIM), F32)],
        compiler_params=_cparams(2),
        name="dilated",
    )(*([P3] * 9), pos3, invf_row)


def _mix_kernel(oa_ref, ob_ref, ga_ref, gb_ref, x_ref, wa_ref, wb_ref, wm_ref, g_ref, b_ref, o_ref):
    ya = _dot(oa_ref[...], wa_ref[...])
    yb = _dot(ob_ref[...], wb_ref[...])
    h = _sigmoid(ga_ref[...]) * ya + _sigmoid(gb_ref[...]) * yb
    mix = _dot(h.astype(BF16), wm_ref[...])
    o_ref[...] = _layer_norm(ALPHA * x_ref[...] + mix, g_ref[...], b_ref[...])


def _const_spec(shape):
    return pl.BlockSpec(shape, lambda *_: (0,) * len(shape), pipeline_mode=pl.Buffered(1))


def _mix(oa, ob, P, x2d, wa, wb, wm, g, b, tm):
    T = x2d.shape[0]
    row = lambda w, j=0: pl.BlockSpec((tm, w), lambda i: (i, j))
    return pl.pallas_call(
        _mix_kernel,
        grid=(T // tm,),
        in_specs=[row(DN_W), row(DIL_OUT), row(D_MODEL, COL_GA // D_MODEL), row(D_MODEL, COL_GB // D_MODEL),
                  row(D_MODEL), _const_spec(wa.shape), _const_spec(wb.shape), _const_spec(wm.shape),
                  _const_spec(g.shape), _const_spec(b.shape)],
        out_specs=row(D_MODEL),
        out_shape=jax.ShapeDtypeStruct((T, D_MODEL), F32),
        compiler_params=_cparams(1),
        name="mix",
    )(oa, ob, P, P, x2d, wa, wb, wm, g, b)


def _memkv_kernel(m_ref, w_ref, o_ref):
    o_ref[...] = _dot(m_ref[...].astype(BF16), w_ref[...]).astype(o_ref.dtype)


def _mem_kv(mem2d, wkv, tm):
    R = mem2d.shape[0]
    return pl.pallas_call(
        _memkv_kernel,
        grid=(R // tm,),
        in_specs=[pl.BlockSpec((tm, D_MODEL), lambda i: (i, 0)), _const_spec(wkv.shape)],
        out_specs=pl.BlockSpec((tm, 2 * D_MODEL), lambda i: (i, 0)),
        out_shape=jax.ShapeDtypeStruct((R, 2 * D_MODEL), BF16),
        compiler_params=_cparams(1),
        name="mem_kv",
    )(mem2d, wkv)


def _xattn_kernel(x_ref, kv_ref, wq_ref, wo_ref, g_ref, b_ref, o_ref):
    x = x_ref[...]
    q = _dot(x.astype(BF16), wq_ref[...])
    scale = MEM_HEAD_DIM ** -0.5
    outs = []
    for h in range(MEM_HEADS):
        cols = slice(h * MEM_HEAD_DIM, (h + 1) * MEM_HEAD_DIM)
        vcols = slice(D_MODEL + h * MEM_HEAD_DIM, D_MODEL + (h + 1) * MEM_HEAD_DIM)
        s = _dot_nt(q[:, cols].astype(BF16), kv_ref[:, cols]) * scale
        p = jnp.exp(s - jnp.max(s, axis=-1, keepdims=True))
        p = p * (1.0 / jnp.sum(p, axis=-1, keepdims=True))
        outs.append(_dot(p.astype(BF16), kv_ref[:, vcols]).astype(BF16))
    cross = _dot(jnp.concatenate(outs, axis=1), wo_ref[...])
    o_ref[...] = _layer_norm(ALPHA * x + cross, g_ref[...], b_ref[...])


def _xattn(x3, kv3, wq, wo, g, b, tm):
    B, S, _ = x3.shape
    M = kv3.shape[1]
    return pl.pallas_call(
        _xattn_kernel,
        grid=(B, S // tm),
        in_specs=[pl.BlockSpec((None, tm, D_MODEL), lambda bi, i: (bi, i, 0)),
                  pl.BlockSpec((None, M, 2 * D_MODEL), lambda bi, i: (bi, 0, 0)),
                  _const_spec(wq.shape), _const_spec(wo.shape), _const_spec(g.shape), _const_spec(b.shape)],
        out_specs=pl.BlockSpec((None, tm, D_MODEL), lambda bi, i: (bi, i, 0)),
        out_shape=jax.ShapeDtypeStruct((B, S, D_MODEL), F32),
        compiler_params=_cparams(2),
        name="xattn",
    )(x3, kv3, wq, wo, g, b)


def _ffn_kernel(x_ref, w13_ref, w2_ref, g_ref, b_ref, o_ref):
    x = x_ref[...]
    xb = x.astype(BF16)
    acc = jnp.zeros(x.shape, F32)
    for j in range(D_FF // FF_CHUNK):
        ca = slice(j * FF_CHUNK, (j + 1) * FF_CHUNK)
        cb = slice(D_FF + j * FF_CHUNK, D_FF + (j + 1) * FF_CHUNK)
        a = _dot(xb, w13_ref[:, ca])
        h = (a * _sigmoid(a)) * _dot(xb, w13_ref[:, cb])
        acc = acc + _dot(h.astype(BF16), w2_ref[ca, :])
    o_ref[...] = _layer_norm(ALPHA * x + acc, g_ref[...], b_ref[...])


def _ffn(x2d, w13, w2, g, b, tm):
    T = x2d.shape[0]
    return pl.pallas_call(
        _ffn_kernel,
        grid=(T // tm,),
        in_specs=[pl.BlockSpec((tm, D_MODEL), lambda i: (i, 0)),
                  _const_spec(w13.shape), _const_spec(w2.shape), _const_spec(g.shape), _const_spec(b.shape)],
        out_specs=pl.BlockSpec((tm, D_MODEL), lambda i: (i, 0)),
        out_shape=jax.ShapeDtypeStruct((T, D_MODEL), F32),
        compiler_params=_cparams(1),
        name="ffn",
    )(x2d, w13, w2, g, b)


def _relayout_w_in(w):
    sizes = (3 * DN_W, DN_W, DN_HEADS, DN_HEADS, DIL_W, DIL_W, DIL_W, D_MODEL, D_MODEL)
    offs = [0]
    for s in sizes:
        offs.append(offs[-1] + s)
    part = lambda i: w[:, offs[i]:offs[i + 1]]
    used = COL_GATES + 2 * DN_HEADS
    pad = jnp.zeros((w.shape[0], NP - used), w.dtype)
    return jnp.concatenate([part(0), part(1), part(7), part(8), part(4), part(5), part(6),
                            part(2), part(3), pad], axis=1)


def _pad_row(v, fill=0.0):
    return jnp.concatenate([v.astype(F32), jnp.full((LANES - v.shape[0],), fill, F32)])[None, :]


def kernel(x, mem, positions, w_in, conv_w, a_log, dt_bias, dn_norm_g, w_br_a, w_br_b, w_mix_out,
           ln1_g, ln1_b, w_xq, w_xkv, w_xo, ln2_g, ln2_b, w_ffn13, w_ffn2, ln3_g, ln3_b):
    B, S, D = x.shape
    T = B * S
    M = mem.shape[1]
    assert D == D_MODEL and w_in.shape[0] == 1 and S % (DIL_BLOCK * DIL_GROUPS[-1][1]) == 0
    x2d = x.reshape(T, D)
    bf = lambda w: w[0].astype(BF16)
    row = lambda v: v[0][None, :].astype(F32)

    P = _in_proj(x2d.astype(BF16), _relayout_w_in(w_in[0]).astype(BF16), tm=S)
    P3 = P.reshape(B, S, NP)

    o_a = _deltanet(P3, conv_w[0], _pad_row(a_log[0]), _pad_row(dt_bias[0]), row(dn_norm_g))

    inv_freq = ROPE_THETA ** (-jnp.arange(0, HEAD_DIM, 2, dtype=F32) / HEAD_DIM)
    invf_row = jnp.concatenate([inv_freq, inv_freq])[None, :]
    o_b = _dilated(P3, positions.reshape(B, S, 1), invf_row)

    x1 = _mix(o_a.reshape(T, DN_W), o_b.reshape(T, DIL_OUT), P, x2d, bf(w_br_a), bf(w_br_b), bf(w_mix_out),
              row(ln1_g), row(ln1_b), tm=512)

    kv = _mem_kv(mem.reshape(B * M, D), bf(w_xkv), tm=512)
    x2 = _xattn(x1.reshape(B, S, D), kv.reshape(B, M, 2 * D), bf(w_xq), bf(w_xo), row(ln2_g), row(ln2_b), tm=512)

    out = _ffn(x2.reshape(T, D), bf(w_ffn13), bf(w_ffn2), row(ln3_g), row(ln3_b), tm=512)
    return out.reshape(B, S, D)
```

```python
import functools
import math

import jax
import jax.numpy as jnp
from jax import lax
from jax.experimental import pallas as pl
from jax.experimental.pallas import tpu as pltpu

F32 = jnp.float32
BF16 = jnp.bfloat16

LANES = 128
D_MODEL = 1024
DN_HEADS = 8
HEAD_DIM = 128
DN_W = DN_HEADS * HEAD_DIM
DN_CONV = 4
DN_BLOCK = 128
DN_HPG = 2
INV_BASE = 16
DIL_GROUPS = ((128, 1), (512, 4), (2048, 16))
DIL_HPG = 4
DIL_W = len(DIL_GROUPS) * DIL_HPG * HEAD_DIM
DIL_OUT = DIL_HPG * HEAD_DIM
DIL_BLOCK = 128
DIL_UNROLL = 4
ROPE_THETA = 10000.0
MEM_HEADS = 4
MEM_HEAD_DIM = D_MODEL // MEM_HEADS
D_FF = 2816
FF_CHUNK = 256
ALPHA = 2.0 ** 0.25
LN_EPS = 1e-5
RMS_EPS = 1e-6

COL_QKV = 0
COL_Z = 3 * DN_W
COL_GA = COL_Z + DN_W
COL_GB = COL_GA + D_MODEL
COL_BQ = COL_GB + D_MODEL
COL_BK = COL_BQ + DIL_W
COL_BV = COL_BK + DIL_W
COL_GATES = COL_BV + DIL_W
PROJ_TN = 1024
NP = 11264

VMEM_LIMIT = 52 * 1024 * 1024


def _cparams(n_axes):
    return pltpu.CompilerParams(dimension_semantics=("arbitrary",) * n_axes,
                                vmem_limit_bytes=VMEM_LIMIT)


def _dot(a, b):
    return jnp.dot(a, b, preferred_element_type=F32)


def _dot_nt(a, b):
    return lax.dot_general(a, b, (((1,), (1,)), ((), ())), preferred_element_type=F32)


def _sigmoid(x):
    return 1.0 / (1.0 + jnp.exp(-x))


def _silu(x):
    return x * _sigmoid(x)


def _layer_norm(y, g, b):
    mu = jnp.mean(y, axis=-1, keepdims=True)
    yc = y - mu
    var = jnp.mean(yc * yc, axis=-1, keepdims=True)
    return yc * lax.rsqrt(var + LN_EPS) * g + b


def _proj_kernel(x_ref, w_ref, o_ref):
    o_ref[...] = _dot(x_ref[...].astype(BF16), w_ref[...])


def _in_proj(x2d, wb, tm):
    T = x2d.shape[0]
    return pl.pallas_call(
        _proj_kernel,
        grid=(T // tm, NP // PROJ_TN),
        in_specs=[pl.BlockSpec((tm, D_MODEL), lambda i, j: (i, 0)),
                  pl.BlockSpec((D_MODEL, PROJ_TN), lambda i, j: (0, j))],
        out_specs=pl.BlockSpec((tm, PROJ_TN), lambda i, j: (i, j)),
        out_shape=jax.ShapeDtypeStruct((T, NP), F32),
        compiler_params=_cparams(2),
        name="in_proj",
    )(x2d, wb)


def _conv_silu_block(x_ref, w, base, first):
    halo = 8
    if first:
        cur = x_ref[pl.ds(0, DN_BLOCK), :]
        xw = jnp.concatenate([jnp.zeros((halo, cur.shape[1]), cur.dtype), cur], axis=0)
    else:
        xw = x_ref[pl.ds(pl.multiple_of(base - halo, halo), DN_BLOCK + halo), :]
    out = xw[halo:, :] * w[DN_CONV - 1:DN_CONV, :]
    for j in range(DN_CONV - 1):
        lo = halo - (DN_CONV - 1 - j)
        out = out + xw[lo:lo + DN_BLOCK, :] * w[j:j + 1, :]
    return _silu(out)


def _l2norm(x):
    return x * lax.rsqrt(jnp.sum(x * x, axis=-1, keepdims=True) + RMS_EPS)


def _deltanet_kernel(q_ref, k_ref, v_ref, z_ref, gates_ref, cwq_ref, cwk_ref, cwv_ref,
                     alog_ref, dtb_ref, ng_ref, o_ref,
                     gam_s, beta_s, n_s, t_s, p_s, qk_s, rhs_s, mq_s, kdt_s, qd_s, gt_s):
    S = q_ref.shape[0]
    nblk = S // DN_BLOCK
    hg = pl.program_id(1)
    blk = DN_BLOCK
    bc_s, o0_s = n_s, t_s

    @pl.when(hg == 0)
    def _():
        G = gates_ref[...]
        zz = G + dtb_ref[...]
        softplus = jnp.maximum(zz, 0.0) + jnp.log1p(jnp.exp(-jnp.abs(zz)))
        gam = -jnp.exp(alog_ref[...]) * softplus
        rowc = lax.broadcasted_iota(jnp.int32, G.shape, 0) & (blk - 1)
        sh = 1
        while sh < blk:
            gam = gam + jnp.where(rowc >= sh, pltpu.roll(gam, sh, 0), 0.0)
            sh *= 2
        gam_s[...] = gam
        beta_s[...] = _sigmoid(G)

    ri = lax.broadcasted_iota(jnp.int32, (blk, blk), 0)
    ci = lax.broadcasted_iota(jnp.int32, (blk, blk), 1)

    def same_block(b):
        s = int(math.log2(b))
        return (ri >> s) == (ci >> s)

    def setup(c, first):
        base = c * blk
        rows = pl.ds(pl.multiple_of(base, blk), blk)
        conv = lambda r, w: _conv_silu_block(r, w[...], base, first)
        qc, kc, vc = conv(q_ref, cwq_ref), conv(k_ref, cwk_ref), conv(v_ref, cwv_ref)
        gsl = gam_s[rows, :]
        bsl = beta_s[rows, :]
        for i in range(DN_HPG):
            h = hg * DN_HPG + i
            cols = slice(i * HEAD_DIM, (i + 1) * HEAD_DIM)
            gam_col = jnp.sum(jnp.where(ci == h, gsl, 0.0), axis=1, keepdims=True)
            beta_col = jnp.sum(jnp.where(ci == h + DN_HEADS, bsl, 0.0), axis=1, keepdims=True)
            gb = jnp.broadcast_to(gam_col, (blk, HEAD_DIM))
            kn = _l2norm(kc[:, cols])
            qn = _l2norm(qc[:, cols]) * (HEAD_DIM ** -0.5)
            kb = kn * beta_col
            dec = jnp.exp(jnp.where(ri >= ci, gb - gb.T, -jnp.inf))
            a = _dot_nt(jnp.concatenate([qn, kb], axis=0).astype(BF16), kn.astype(BF16))
            qk_s[i, c] = (a[:blk] * dec).astype(BF16)
            n_s[i, c] = -a[blk:] * jnp.where(ri > ci, dec, 0.0)
            eg = jnp.exp(gb)
            rhs_s[i, c] = jnp.concatenate([vc[:, cols] * beta_col, kb * eg], axis=1).astype(BF16)
            qd_s[i, c] = qn * eg
            gl = jnp.broadcast_to(gb[blk - 1:blk, :], (blk, HEAD_DIM))
            kdt_s[i, c] = (kn * jnp.exp(gl - gb)).T.astype(BF16)
            gt_s[i, c] = jnp.exp(gl[:8, :])

    setup(0, True)

    def setup_rest(c, carry):
        setup(c, False)
        return carry

    lax.fori_loop(1, nblk, setup_rest, 0)

    def stage(issue, finish):
        probs = [(i, c) for c in range(nblk) for i in range(DN_HPG)]
        results = [issue(i, c) for i, c in probs]
        for (i, c), r in zip(probs, results):
            finish(i, c, r)

    def series_start(i, c):
        d = jnp.where(same_block(INV_BASE), n_s[i, c], 0.0)
        return d, _dot(d.astype(BF16), d.astype(BF16))

    def series_start_done(i, c, r):
        t_s[i, c] = (ri == ci).astype(F32) + r[0]
        p_s[i, c] = r[1].astype(BF16)

    stage(series_start, series_start_done)

    def series_both(i, c):
        pb = p_s[i, c]
        return _dot(jnp.concatenate([t_s[i, c].astype(BF16), pb], axis=0), pb)

    def series_both_done(i, c, r):
        t_s[i, c] = t_s[i, c] + r[:blk]
        p_s[i, c] = r[blk:].astype(BF16)

    def times_p(i, c):
        return _dot(t_s[i, c].astype(BF16), p_s[i, c])

    def add_to_t(i, c, r):
        t_s[i, c] = t_s[i, c] + r

    for _ in range(int(math.log2(INV_BASE)) - 2):
        stage(series_both, series_both_done)
    stage(times_p, add_to_t)

    b = INV_BASE
    while b < blk:
        off = same_block(2 * b) & jnp.logical_not(same_block(b))

        def off_times_t(i, c, off=off):
            return _dot(jnp.where(off, n_s[i, c], 0.0).astype(BF16), t_s[i, c].astype(BF16))

        def to_p(i, c, r):
            p_s[i, c] = r.astype(BF16)

        stage(off_times_t, to_p)
        stage(times_p, add_to_t)
        b *= 2

    def solve(i, c):
        return _dot(t_s[i, c].astype(BF16), rhs_s[i, c])

    def solve_done(i, c, r):
        rhs_s[i, c] = r.astype(BF16)

    stage(solve, solve_done)

    def cross(i, c):
        return _dot(jnp.concatenate([kdt_s[i, c], qk_s[i, c]], axis=0), rhs_s[i, c])

    def cross_done(i, c, r):
        bc_s[i, c] = r[:blk, :HEAD_DIM]
        o0_s[i, c] = r[blk:, :HEAD_DIM]
        mq_s[i, c] = jnp.concatenate([r[:blk, HEAD_DIM:], qd_s[i, c] - r[blk:, HEAD_DIM:]], axis=0).astype(BF16)

    stage(cross, cross_done)

    ng = ng_ref[...]

    def scan(c, states):
        rows = pl.ds(pl.multiple_of(c * blk, blk), blk)
        new_states = []
        for i in range(DN_HPG):
            cols = slice(i * HEAD_DIM, (i + 1) * HEAD_DIM)
            st = states[i]
            r = _dot(mq_s[i, c], st.astype(BF16))
            g_tot = jnp.broadcast_to(gt_s[i, c][0:1, :], (HEAD_DIM, HEAD_DIM))
            new_states.append(g_tot * st + bc_s[i, c] - r[:blk])
            o = o0_s[i, c] + r[blk:]
            o = o * lax.rsqrt(jnp.mean(o * o, axis=-1, keepdims=True) + RMS_EPS) * ng
            o_ref[rows, cols] = (o * _silu(z_ref[rows, cols])).astype(o_ref.dtype)
        return tuple(new_states)

    init = tuple(jnp.zeros((HEAD_DIM, HEAD_DIM), F32) for _ in range(DN_HPG))
    lax.fori_loop(0, nblk, scan, init)


def _deltanet(P3, conv_w, alog_row, dtb_row, ng_row):
    B, S, _ = P3.shape
    W = DN_HPG * HEAD_DIM
    ngrp = DN_HEADS // DN_HPG
    nblk = S // DN_BLOCK

    def col_spec(base):
        return pl.BlockSpec((None, S, W), lambda b, g: (b, 0, base // W + g))

    def cw_spec(base):
        return pl.BlockSpec((DN_CONV, W), lambda b, g: (0, base // W + g))

    row_spec = pl.BlockSpec((1, LANES), lambda b, g: (0, 0))
    blk_buf = lambda rows, cols, dt: pltpu.VMEM((DN_HPG, nblk, rows, cols), dt)
    return pl.pallas_call(
        _deltanet_kernel,
        grid=(B, ngrp),
        in_specs=[col_spec(COL_QKV), col_spec(COL_QKV + DN_W), col_spec(COL_QKV + 2 * DN_W), col_spec(COL_Z),
                  pl.BlockSpec((None, S, LANES), lambda b, g: (b, 0, COL_GATES // LANES)),
                  cw_spec(0), cw_spec(DN_W), cw_spec(2 * DN_W),
                  row_spec, row_spec, row_spec],
        out_specs=pl.BlockSpec((None, S, W), lambda b, g: (b, 0, g)),
        out_shape=jax.ShapeDtypeStruct((B, S, DN_W), BF16),
        scratch_shapes=[pltpu.VMEM((S, LANES), F32), pltpu.VMEM((S, LANES), F32),
                        blk_buf(DN_BLOCK, DN_BLOCK, F32), blk_buf(DN_BLOCK, DN_BLOCK, F32),
                        blk_buf(DN_BLOCK, DN_BLOCK, BF16), blk_buf(DN_BLOCK, DN_BLOCK, BF16),
                        blk_buf(DN_BLOCK, 2 * HEAD_DIM, BF16),
                        blk_buf(2 * DN_BLOCK, HEAD_DIM, BF16),
                        blk_buf(HEAD_DIM, DN_BLOCK, BF16),
                        blk_buf(DN_BLOCK, HEAD_DIM, F32),
                        blk_buf(8, HEAD_DIM, F32)],
        compiler_params=_cparams(2),
        name="deltanet",
    )(P3, P3, P3, P3, P3, conv_w, conv_w, conv_w, alog_row, dtb_row, ng_row)


def _dilated_kernel(q0_ref, q1_ref, q2_ref, k0_ref, k1_ref, k2_ref, v0_ref, v1_ref, v2_ref,
                    pos_ref, invf_ref, o_ref, cos_s, sin_s, qr_s, kr_s, og_s, lg_s):
    S = q0_ref.shape[0]
    blk = DIL_BLOCK
    scale = HEAD_DIM ** -0.5

    @pl.when(pl.program_id(1) == 0)
    def _():
        ang = pos_ref[...].astype(F32) * invf_ref[...]
        lane = lax.broadcasted_iota(jnp.int32, ang.shape, 1)
        cos_s[...] = jnp.cos(ang)
        sin_s[...] = jnp.where(lane < HEAD_DIM // 2, -jnp.sin(ang), jnp.sin(ang))

    def rope(x):
        return x * cos_s[...] + pltpu.roll(x, HEAD_DIM // 2, 1) * sin_s[...]

    ri = lax.broadcasted_iota(jnp.int32, (blk, blk), 0)
    ci = lax.broadcasted_iota(jnp.int32, (blk, blk), 1)
    neg = -jnp.inf

    def rows_of(start, dil):
        return pl.ds(start, blk) if dil == 1 else pl.ds(start, blk, stride=dil)

    def block_group(g, v_ref, starts, dil, has_prev):
        rows = [rows_of(st, dil) for st in starts]
        q = [qr_s[r, :].astype(BF16) for r in rows]
        s = [_dot_nt(q[j], kr_s[r, :].astype(BF16)) for j, r in enumerate(rows)]
        if has_prev:
            prows = [rows_of(st - blk * dil, dil) for st in starts]
            sp = [_dot_nt(q[j], kr_s[r, :].astype(BF16)) for j, r in enumerate(prows)]
        ps, pps, lses = [], [], []
        for j in range(len(starts)):
            sj = jnp.where(ri >= ci, s[j] * scale, neg)
            m = jnp.max(sj, axis=-1, keepdims=True)
            if has_prev:
                spj = jnp.where(ci >= ri, sp[j] * scale, neg)
                m = jnp.maximum(m, jnp.max(spj, axis=-1, keepdims=True))
                pp = jnp.exp(spj - m)
            p = jnp.exp(sj - m)
            l = jnp.sum(p, axis=-1, keepdims=True)
            if has_prev:
                l = l + jnp.sum(pp, axis=-1, keepdims=True)
            inv = 1.0 / l
            ps.append((p * inv).astype(BF16))
            if has_prev:
                pps.append((pp * inv).astype(BF16))
            lses.append(m + jnp.log(l))
        for j, r in enumerate(rows):
            o = _dot(ps[j], v_ref[r, :].astype(BF16))
            if has_prev:
                o = o + _dot(pps[j], v_ref[prows[j], :].astype(BF16))
            og_s[g, r, :] = o
            lg_s[g, r, :] = jnp.broadcast_to(lses[j], (blk, HEAD_DIM))

    q_refs = (q0_ref, q1_ref, q2_ref)
    k_refs = (k0_ref, k1_ref, k2_ref)
    v_refs = (v0_ref, v1_ref, v2_ref)
    for g, (window, dil) in enumerate(DIL_GROUPS):
        assert window // dil == blk
        qr_s[...] = rope(q_refs[g][...])
        kr_s[...] = rope(k_refs[g][...])
        nb = S // dil // blk
        v_ref = v_refs[g]

        u0 = min(dil, DIL_UNROLL)

        def first(it, carry, g=g, v_ref=v_ref, dil=dil, u0=u0):
            block_group(g, v_ref, [it * u0 + j for j in range(u0)], dil, False)
            return carry

        lax.fori_loop(0, dil // u0, first, 0)
        n_later = (nb - 1) * dil
        if n_later:
            u1 = max(u for u in range(1, DIL_UNROLL + 1) if n_later % u == 0)

            def later(it, carry, g=g, v_ref=v_ref, dil=dil, u1=u1):
                starts = []
                for j in range(u1):
                    idx = it * u1 + j
                    starts.append((idx // dil + 1) * (blk * dil) + idx % dil)
                block_group(g, v_ref, starts, dil, True)
                return carry

            lax.fori_loop(0, n_later // u1, later, 0)

    l0, l1, l2 = lg_s[0], lg_s[1], lg_s[2]
    m = jnp.maximum(jnp.maximum(l0, l1), l2)
    e0, e1, e2 = jnp.exp(l0 - m), jnp.exp(l1 - m), jnp.exp(l2 - m)
    inv = 1.0 / (e0 + e1 + e2)
    o_ref[...] = ((e0 * inv) * og_s[0] + (e1 * inv) * og_s[1] + (e2 * inv) * og_s[2]).astype(o_ref.dtype)


def _dilated(P3, pos3, invf_row):
    B, S, _ = P3.shape

    def head_spec(base, g):
        return pl.BlockSpec((None, S, HEAD_DIM), lambda b, h: (b, 0, base // HEAD_DIM + g * DIL_HPG + h))

    ngr = len(DIL_GROUPS)
    in_specs = ([head_spec(COL_BQ, g) for g in range(ngr)] + [head_spec(COL_BK, g) for g in range(ngr)]
                + [head_spec(COL_BV, g) for g in range(ngr)]
                + [pl.BlockSpec((None, S, 1), lambda b, h: (b, 0, 0)),
                   pl.BlockSpec((1, HEAD_DIM), lambda b, h: (0, 0))])
    big = pltpu.VMEM((S, HEAD_DIM), F32)
    return pl.pallas_call(
        _dilated_kernel,
        grid=(B, DIL_HPG),
        in_specs=in_specs,
        out_specs=pl.BlockSpec((None, S, HEAD_DIM), lambda b, h: (b, 0, h)),
        out_shape=jax.ShapeDtypeStruct((B, S, DIL_OUT), BF16),
        scratch_shapes=[big, big, big, big,
                        pltpu.VMEM((ngr, S, HEAD_DIM), F32), pltpu.VMEM((ngr, S, HEAD_DIM), F32)],
        compiler_params=_cparams(2),
        name="dilated",
    )(*([P3] * 9), pos3, invf_row)


def _mix_kernel(oa_ref, ob_ref, ga_ref, gb_ref, x_ref, wa_ref, wb_ref, wm_ref, g_ref, b_ref, o_ref):
    ya = _dot(oa_ref[...], wa_ref[...])
    yb = _dot(ob_ref[...], wb_ref[...])
    h = _sigmoid(ga_ref[...]) * ya + _sigmoid(gb_ref[...]) * yb
    mix = _dot(h.astype(BF16), wm_ref[...])
    o_ref[...] = _layer_norm(ALPHA * x_ref[...] + mix, g_ref[...], b_ref[...])


def _const_spec(shape):
    return pl.BlockSpec(shape, lambda *_: (0,) * len(shape), pipeline_mode=pl.Buffered(1))


def _mix(oa, ob, P, x2d, wa, wb, wm, g, b, tm):
    T = x2d.shape[0]
    row = lambda w, j=0: pl.BlockSpec((tm, w), lambda i: (i, j))
    return pl.pallas_call(
        _mix_kernel,
        grid=(T // tm,),
        in_specs=[row(DN_W), row(DIL_OUT), row(D_MODEL, COL_GA // D_MODEL), row(D_MODEL, COL_GB // D_MODEL),
                  row(D_MODEL), _const_spec(wa.shape), _const_spec(wb.shape), _const_spec(wm.shape),
                  _const_spec(g.shape), _const_spec(b.shape)],
        out_specs=row(D_MODEL),
        out_shape=jax.ShapeDtypeStruct((T, D_MODEL), F32),
        compiler_params=_cparams(1),
        name="mix",
    )(oa, ob, P, P, x2d, wa, wb, wm, g, b)


def _memkv_kernel(m_ref, w_ref, o_ref):
    o_ref[...] = _dot(m_ref[...].astype(BF16), w_ref[...]).astype(o_ref.dtype)


def _mem_kv(mem2d, wkv, tm):
    R = mem2d.shape[0]
    return pl.pallas_call(
        _memkv_kernel,
        grid=(R // tm,),
        in_specs=[pl.BlockSpec((tm, D_MODEL), lambda i: (i, 0)), _const_spec(wkv.shape)],
        out_specs=pl.BlockSpec((tm, 2 * D_MODEL), lambda i: (i, 0)),
        out_shape=jax.ShapeDtypeStruct((R, 2 * D_MODEL), BF16),
        compiler_params=_cparams(1),
        name="mem_kv",
    )(mem2d, wkv)


def _xattn_kernel(x_ref, kv_ref, wq_ref, wo_ref, g_ref, b_ref, o_ref):
    x = x_ref[...]
    q = _dot(x.astype(BF16), wq_ref[...])
    scale = MEM_HEAD_DIM ** -0.5
    outs = []
    for h in range(MEM_HEADS):
        cols = slice(h * MEM_HEAD_DIM, (h + 1) * MEM_HEAD_DIM)
        vcols = slice(D_MODEL + h * MEM_HEAD_DIM, D_MODEL + (h + 1) * MEM_HEAD_DIM)
        s = _dot_nt(q[:, cols].astype(BF16), kv_ref[:, cols]) * scale
        p = jnp.exp(s - jnp.max(s, axis=-1, keepdims=True))
        p = p * (1.0 / jnp.sum(p, axis=-1, keepdims=True))
        outs.append(_dot(p.astype(BF16), kv_ref[:, vcols]).astype(BF16))
    cross = _dot(jnp.concatenate(outs, axis=1), wo_ref[...])
    o_ref[...] = _layer_norm(ALPHA * x + cross, g_ref[...], b_ref[...])


def _xattn(x3, kv3, wq, wo, g, b, tm):
    B, S, _ = x3.shape
    M = kv3.shape[1]
    return pl.pallas_call(
        _xattn_kernel,
        grid=(B, S // tm),
        in_specs=[pl.BlockSpec((None, tm, D_MODEL), lambda bi, i: (bi, i, 0)),
                  pl.BlockSpec((None, M, 2 * D_MODEL), lambda bi, i: (bi, 0, 0)),
                  _const_spec(wq.shape), _const_spec(wo.shape), _const_spec(g.shape), _const_spec(b.shape)],
        out_specs=pl.BlockSpec((None, tm, D_MODEL), lambda bi, i: (bi, i, 0)),
        out_shape=jax.ShapeDtypeStruct((B, S, D_MODEL), F32),
        compiler_params=_cparams(2),
        name="xattn",
    )(x3, kv3, wq, wo, g, b)


def _ffn_kernel(x_ref, w13_ref, w2_ref, g_ref, b_ref, o_ref):
    x = x_ref[...]
    xb = x.astype(BF16)
    acc = jnp.zeros(x.shape, F32)
    for j in range(D_FF // FF_CHUNK):
        ca = slice(j * FF_CHUNK, (j + 1) * FF_CHUNK)
        cb = slice(D_FF + j * FF_CHUNK, D_FF + (j + 1) * FF_CHUNK)
        a = _dot(xb, w13_ref[:, ca])
        h = (a * _sigmoid(a)) * _dot(xb, w13_ref[:, cb])
        acc = acc + _dot(h.astype(BF16), w2_ref[ca, :])
    o_ref[...] = _layer_norm(ALPHA * x + acc, g_ref[...], b_ref[...])


def _ffn(x2d, w13, w2, g, b, tm):
    T = x2d.shape[0]
    return pl.pallas_call(
        _ffn_kernel,
        grid=(T // tm,),
        in_specs=[pl.BlockSpec((tm, D_MODEL), lambda i: (i, 0)),
                  _const_spec(w13.shape), _const_spec(w2.shape), _const_spec(g.shape), _const_spec(b.shape)],
        out_specs=pl.BlockSpec((tm, D_MODEL), lambda i: (i, 0)),
        out_shape=jax.ShapeDtypeStruct((T, D_MODEL), F32),
        compiler_params=_cparams(1),
        name="ffn",
    )(x2d, w13, w2, g, b)


def _relayout_w_in(w):
    sizes = (3 * DN_W, DN_W, DN_HEADS, DN_HEADS, DIL_W, DIL_W, DIL_W, D_MODEL, D_MODEL)
    offs = [0]
    for s in sizes:
        offs.append(offs[-1] + s)
    part = lambda i: w[:, offs[i]:offs[i + 1]]
    used = COL_GATES + 2 * DN_HEADS
    pad = jnp.zeros((w.shape[0], NP - used), w.dtype)
    return jnp.concatenate([part(0), part(1), part(7), part(8), part(4), part(5), part(6),
                            part(2), part(3), pad], axis=1)


def _pad_row(v, fill=0.0):
    return jnp.concatenate([v.astype(F32), jnp.full((LANES - v.shape[0],), fill, F32)])[None, :]


def kernel(x, mem, positions, w_in, conv_w, a_log, dt_bias, dn_norm_g, w_br_a, w_br_b, w_mix_out,
           ln1_g, ln1_b, w_xq, w_xkv, w_xo, ln2_g, ln2_b, w_ffn13, w_ffn2, ln3_g, ln3_b):
    B, S, D = x.shape
    T = B * S
    M = mem.shape[1]
    assert D == D_MODEL and w_in.shape[0] == 1 and S % (DIL_BLOCK * DIL_GROUPS[-1][1]) == 0
    x2d = x.reshape(T, D)
    bf = lambda w: w[0].astype(BF16)
    row = lambda v: v[0][None, :].astype(F32)

    P = _in_proj(x2d, _relayout_w_in(w_in[0]).astype(BF16), tm=S)
    P3 = P.reshape(B, S, NP)

    o_a = _deltanet(P3, conv_w[0], _pad_row(a_log[0]), _pad_row(dt_bias[0]), row(dn_norm_g))

    inv_freq = ROPE_THETA ** (-jnp.arange(0, HEAD_DIM, 2, dtype=F32) / HEAD_DIM)
    invf_row = jnp.concatenate([inv_freq, inv_freq])[None, :]
    o_b = _dilated(P3, positions.reshape(B, S, 1), invf_row)

    x1 = _mix(o_a.reshape(T, DN_W), o_b.reshape(T, DIL_OUT), P, x2d, bf(w_br_a), bf(w_br_b), bf(w_mix_out),
              row(ln1_g), row(ln1_b), tm=512)

    kv = _mem_kv(mem.reshape(B * M, D), bf(w_xkv), tm=512)
    x2 = _xattn(x1.reshape(B, S, D), kv.reshape(B, M, 2 * D), bf(w_xq), bf(w_xo), row(ln2_g), row(ln2_b), tm=512)

    out = _ffn(x2.reshape(T, D), bf(w_ffn13), bf(w_ffn2), row(ln3_g), row(ln3_b), tm=512)
    return out.reshape(B, S, D)
```

```python
import functools
import math

import jax
import jax.numpy as jnp
from jax import lax
from jax.experimental import pallas as pl
from jax.experimental.pallas import tpu as pltpu

F32 = jnp.float32
BF16 = jnp.bfloat16

LANES = 128
D_MODEL = 1024
DN_HEADS = 8
HEAD_DIM = 128
DN_W = DN_HEADS * HEAD_DIM
DN_CONV = 4
DN_BLOCK = 128
DN_HPG = 2
INV_BASE = 16
DIL_GROUPS = ((128, 1), (512, 4), (2048, 16))
DIL_HPG = 4
DIL_W = len(DIL_GROUPS) * DIL_HPG * HEAD_DIM
DIL_OUT = DIL_HPG * HEAD_DIM
DIL_BLOCK = 128
ROPE_THETA = 10000.0
MEM_HEADS = 4
MEM_HEAD_DIM = D_MODEL // MEM_HEADS
D_FF = 2816
FF_CHUNK = 256
ALPHA = 2.0 ** 0.25
LN_EPS = 1e-5
RMS_EPS = 1e-6

COL_QKV = 0
COL_Z = 3 * DN_W
COL_GA = COL_Z + DN_W
COL_GB = COL_GA + D_MODEL
COL_BQ = COL_GB + D_MODEL
COL_BK = COL_BQ + DIL_W
COL_BV = COL_BK + DIL_W
COL_GATES = COL_BV + DIL_W
PROJ_TN = 1024
NP = 11264

VMEM_LIMIT = 52 * 1024 * 1024


def _cparams(n_axes):
    return pltpu.CompilerParams(dimension_semantics=("arbitrary",) * n_axes,
                                vmem_limit_bytes=VMEM_LIMIT)


def _dot(a, b):
    return jnp.dot(a, b, preferred_element_type=F32)


def _dot_nt(a, b):
    return lax.dot_general(a, b, (((1,), (1,)), ((), ())), preferred_element_type=F32)


def _sigmoid(x):
    return 1.0 / (1.0 + jnp.exp(-x))


def _silu(x):
    return x * _sigmoid(x)


def _layer_norm(y, g, b):
    mu = jnp.mean(y, axis=-1, keepdims=True)
    yc = y - mu
    var = jnp.mean(yc * yc, axis=-1, keepdims=True)
    return yc * lax.rsqrt(var + LN_EPS) * g + b


def _proj_kernel(x_ref, w_ref, o_ref, gates_ref):
    acc = _dot(x_ref[...].astype(BF16), w_ref[...])
    o_ref[...] = acc.astype(o_ref.dtype)

    @pl.when(pl.program_id(1) == COL_GATES // PROJ_TN)
    def _():
        lo = COL_GATES % PROJ_TN
        gates_ref[...] = acc[:, lo:lo + LANES]


def _in_proj(x2d, wb, tm):
    T = x2d.shape[0]
    return pl.pallas_call(
        _proj_kernel,
        grid=(T // tm, NP // PROJ_TN),
        in_specs=[pl.BlockSpec((tm, D_MODEL), lambda i, j: (i, 0)),
                  pl.BlockSpec((D_MODEL, PROJ_TN), lambda i, j: (0, j))],
        out_specs=[pl.BlockSpec((tm, PROJ_TN), lambda i, j: (i, j)),
                   pl.BlockSpec((tm, LANES), lambda i, j: (i, 0))],
        out_shape=[jax.ShapeDtypeStruct((T, NP), BF16), jax.ShapeDtypeStruct((T, LANES), F32)],
        compiler_params=_cparams(2),
        name="in_proj",
    )(x2d, wb)


def _conv_silu_block(x_ref, w, base, first):
    halo = 16
    if first:
        cur = x_ref[pl.ds(0, DN_BLOCK), :].astype(F32)
        xw = jnp.concatenate([jnp.zeros((halo, cur.shape[1]), F32), cur], axis=0)
    else:
        xw = x_ref[pl.ds(pl.multiple_of(base - halo, halo), DN_BLOCK + halo), :].astype(F32)
    out = xw[halo:, :] * w[DN_CONV - 1:DN_CONV, :]
    for j in range(DN_CONV - 1):
        lo = halo - (DN_CONV - 1 - j)
        out = out + xw[lo:lo + DN_BLOCK, :] * w[j:j + 1, :]
    return _silu(out)


def _l2norm(x):
    return x * lax.rsqrt(jnp.sum(x * x, axis=-1, keepdims=True) + RMS_EPS)


def _deltanet_kernel(q_ref, k_ref, v_ref, z_ref, gates_ref, cwq_ref, cwk_ref, cwv_ref,
                     alog_ref, dtb_ref, ng_ref, o_ref,
                     gam_s, beta_s, n_s, t_s, p_s, qk_s, rhs_s, mq_s, kdt_s, qd_s, gt_s):
    S = q_ref.shape[0]
    nblk = S // DN_BLOCK
    hg = pl.program_id(1)
    blk = DN_BLOCK
    bc_s, o0_s = n_s, t_s

    @pl.when(hg == 0)
    def _():
        G = gates_ref[...]
        zz = G + dtb_ref[...]
        softplus = jnp.maximum(zz, 0.0) + jnp.log1p(jnp.exp(-jnp.abs(zz)))
        gam = -jnp.exp(alog_ref[...]) * softplus
        rowc = lax.broadcasted_iota(jnp.int32, G.shape, 0) & (blk - 1)
        sh = 1
        while sh < blk:
            gam = gam + jnp.where(rowc >= sh, pltpu.roll(gam, sh, 0), 0.0)
            sh *= 2
        gam_s[...] = gam
        beta_s[...] = _sigmoid(G)

    ri = lax.broadcasted_iota(jnp.int32, (blk, blk), 0)
    ci = lax.broadcasted_iota(jnp.int32, (blk, blk), 1)

    def same_block(b):
        s = int(math.log2(b))
        return (ri >> s) == (ci >> s)

    def setup(c, first):
        base = c * blk
        rows = pl.ds(pl.multiple_of(base, blk), blk)
        conv = lambda r, w: _conv_silu_block(r, w[...], base, first)
        qc, kc, vc = conv(q_ref, cwq_ref), conv(k_ref, cwk_ref), conv(v_ref, cwv_ref)
        gsl = gam_s[rows, :]
        bsl = beta_s[rows, :]
        for i in range(DN_HPG):
            h = hg * DN_HPG + i
            cols = slice(i * HEAD_DIM, (i + 1) * HEAD_DIM)
            gam_col = jnp.sum(jnp.where(ci == h, gsl, 0.0), axis=1, keepdims=True)
            beta_col = jnp.sum(jnp.where(ci == h + DN_HEADS, bsl, 0.0), axis=1, keepdims=True)
            gb = jnp.broadcast_to(gam_col, (blk, HEAD_DIM))
            kn = _l2norm(kc[:, cols])
            qn = _l2norm(qc[:, cols]) * (HEAD_DIM ** -0.5)
            kb = kn * beta_col
            dec = jnp.exp(jnp.where(ri >= ci, gb - gb.T, -jnp.inf))
            a = _dot_nt(jnp.concatenate([qn, kb], axis=0).astype(BF16), kn.astype(BF16))
            qk_s[i, c] = (a[:blk] * dec).astype(BF16)
            n_s[i, c] = -a[blk:] * jnp.where(ri > ci, dec, 0.0)
            eg = jnp.exp(gb)
            rhs_s[i, c] = jnp.concatenate([vc[:, cols] * beta_col, kb * eg], axis=1).astype(BF16)
            qd_s[i, c] = qn * eg
            gl = jnp.broadcast_to(gb[blk - 1:blk, :], (blk, HEAD_DIM))
            kdt_s[i, c] = (kn * jnp.exp(gl - gb)).T.astype(BF16)
            gt_s[i, c] = jnp.exp(gl[:8, :])

    setup(0, True)

    def setup_rest(c, carry):
        setup(c, False)
        return carry

    lax.fori_loop(1, nblk, setup_rest, 0)

    def stage(issue, finish):
        probs = [(i, c) for c in range(nblk) for i in range(DN_HPG)]
        results = [issue(i, c) for i, c in probs]
        for (i, c), r in zip(probs, results):
            finish(i, c, r)

    def series_start(i, c):
        d = jnp.where(same_block(INV_BASE), n_s[i, c], 0.0)
        return d, _dot(d.astype(BF16), d.astype(BF16))

    def series_start_done(i, c, r):
        t_s[i, c] = (ri == ci).astype(F32) + r[0]
        p_s[i, c] = r[1].astype(BF16)

    stage(series_start, series_start_done)

    def series_both(i, c):
        pb = p_s[i, c]
        return _dot(jnp.concatenate([t_s[i, c].astype(BF16), pb], axis=0), pb)

    def series_both_done(i, c, r):
        t_s[i, c] = t_s[i, c] + r[:blk]
        p_s[i, c] = r[blk:].astype(BF16)

    def times_p(i, c):
        return _dot(t_s[i, c].astype(BF16), p_s[i, c])

    def add_to_t(i, c, r):
        t_s[i, c] = t_s[i, c] + r

    for _ in range(int(math.log2(INV_BASE)) - 2):
        stage(series_both, series_both_done)
    stage(times_p, add_to_t)

    b = INV_BASE
    while b < blk:
        off = same_block(2 * b) & jnp.logical_not(same_block(b))

        def off_times_t(i, c, off=off):
            return _dot(jnp.where(off, n_s[i, c], 0.0).astype(BF16), t_s[i, c].astype(BF16))

        def to_p(i, c, r):
            p_s[i, c] = r.astype(BF16)

        stage(off_times_t, to_p)
        stage(times_p, add_to_t)
        b *= 2

    def solve(i, c):
        return _dot(t_s[i, c].astype(BF16), rhs_s[i, c])

    def solve_done(i, c, r):
        rhs_s[i, c] = r.astype(BF16)

    stage(solve, solve_done)

    def cross(i, c):
        return _dot(jnp.concatenate([kdt_s[i, c], qk_s[i, c]], axis=0), rhs_s[i, c])

    def cross_done(i, c, r):
        bc_s[i, c] = r[:blk, :HEAD_DIM]
        o0_s[i, c] = r[blk:, :HEAD_DIM]
        mq_s[i, c] = jnp.concatenate([r[:blk, HEAD_DIM:], qd_s[i, c] - r[blk:, HEAD_DIM:]], axis=0).astype(BF16)

    stage(cross, cross_done)

    ng = ng_ref[...]

    def scan(c, states):
        rows = pl.ds(pl.multiple_of(c * blk, blk), blk)
        new_states = []
        for i in range(DN_HPG):
            cols = slice(i * HEAD_DIM, (i + 1) * HEAD_DIM)
            st = states[i]
            r = _dot(mq_s[i, c], st.astype(BF16))
            g_tot = jnp.broadcast_to(gt_s[i, c][0:1, :], (HEAD_DIM, HEAD_DIM))
            new_states.append(g_tot * st + bc_s[i, c] - r[:blk])
            o = o0_s[i, c] + r[blk:]
            o = o * lax.rsqrt(jnp.mean(o * o, axis=-1, keepdims=True) + RMS_EPS) * ng
            o_ref[rows, cols] = (o * _silu(z_ref[rows, cols].astype(F32))).astype(o_ref.dtype)
        return tuple(new_states)

    init = tuple(jnp.zeros((HEAD_DIM, HEAD_DIM), F32) for _ in range(DN_HPG))
    lax.fori_loop(0, nblk, scan, init)


def _deltanet(P3, G3, conv_w, alog_row, dtb_row, ng_row):
    B, S, _ = P3.shape
    W = DN_HPG * HEAD_DIM
    ngrp = DN_HEADS // DN_HPG
    nblk = S // DN_BLOCK

    def col_spec(base):
        return pl.BlockSpec((None, S, W), lambda b, g: (b, 0, base // W + g))

    def cw_spec(base):
        return pl.BlockSpec((DN_CONV, W), lambda b, g: (0, base // W + g))

    row_spec = pl.BlockSpec((1, LANES), lambda b, g: (0, 0))
    blk_buf = lambda rows, cols, dt: pltpu.VMEM((DN_HPG, nblk, rows, cols), dt)
    return pl.pallas_call(
        _deltanet_kernel,
        grid=(B, ngrp),
        in_specs=[col_spec(COL_QKV), col_spec(COL_QKV + DN_W), col_spec(COL_QKV + 2 * DN_W), col_spec(COL_Z),
                  pl.BlockSpec((None, S, LANES), lambda b, g: (b, 0, 0)),
                  cw_spec(0), cw_spec(DN_W), cw_spec(2 * DN_W),
                  row_spec, row_spec, row_spec],
        out_specs=pl.BlockSpec((None, S, W), lambda b, g: (b, 0, g)),
        out_shape=jax.ShapeDtypeStruct((B, S, DN_W), BF16),
        scratch_shapes=[pltpu.VMEM((S, LANES), F32), pltpu.VMEM((S, LANES), F32),
                        blk_buf(DN_BLOCK, DN_BLOCK, F32), blk_buf(DN_BLOCK, DN_BLOCK, F32),
                        blk_buf(DN_BLOCK, DN_BLOCK, BF16), blk_buf(DN_BLOCK, DN_BLOCK, BF16),
                        blk_buf(DN_BLOCK, 2 * HEAD_DIM, BF16),
                        blk_buf(2 * DN_BLOCK, HEAD_DIM, BF16),
                        blk_buf(HEAD_DIM, DN_BLOCK, BF16),
                        blk_buf(DN_BLOCK, HEAD_DIM, F32),
                        blk_buf(8, HEAD_DIM, F32)],
        compiler_params=_cparams(2),
        name="deltanet",
    )(P3, P3, P3, P3, G3, conv_w, conv_w, conv_w, alog_row, dtb_row, ng_row)


def _dilated_kernel(q0_ref, q1_ref, q2_ref, k0_ref, k1_ref, k2_ref, v0_ref, v1_ref, v2_ref,
                    pos_ref, invf_ref, o_ref, cos_s, sin_s, qr_s, kr_s, vr_s, og_s, lg_s):
    S = q0_ref.shape[0]
    blk = DIL_BLOCK
    scale = HEAD_DIM ** -0.5

    @pl.when(pl.program_id(1) == 0)
    def _():
        ang = pos_ref[...].astype(F32) * invf_ref[...]
        lane = lax.broadcasted_iota(jnp.int32, ang.shape, 1)
        cos_s[...] = jnp.cos(ang)
        sin_s[...] = jnp.where(lane < HEAD_DIM // 2, -jnp.sin(ang), jnp.sin(ang))

    def rope(x):
        return x * cos_s[...] + pltpu.roll(x, HEAD_DIM // 2, 1) * sin_s[...]

    ri = lax.broadcasted_iota(jnp.int32, (blk, blk), 0)
    ci = lax.broadcasted_iota(jnp.int32, (blk, blk), 1)
    neg = -jnp.inf

    def rows_of(start, dil):
        return pl.ds(start, blk) if dil == 1 else pl.ds(start, blk, stride=dil)

    ones = jnp.ones((blk, HEAD_DIM), BF16)

    def block_group(g, starts, dil, has_prev):
        rows = [rows_of(st, dil) for st in starts]
        q = [qr_s[g, r, :].astype(BF16) for r in rows]
        s = [_dot_nt(q[j], kr_s[g, r, :].astype(BF16)) for j, r in enumerate(rows)]
        if has_prev:
            prows = [rows_of(st - blk * dil, dil) for st in starts]
            sp = [_dot_nt(q[j], kr_s[g, r, :].astype(BF16)) for j, r in enumerate(prows)]
        ps, pps, ms = [], [], []
        for j in range(len(starts)):
            sj = jnp.where(ri >= ci, s[j] * scale, neg)
            if has_prev:
                spj = jnp.where(ci >= ri, sp[j] * scale, neg)
                m = jnp.max(jnp.maximum(sj, spj), axis=-1, keepdims=True)
                pps.append(jnp.exp(spj - m).astype(BF16))
            else:
                m = jnp.max(sj, axis=-1, keepdims=True)
            ps.append(jnp.exp(sj - m).astype(BF16))
            ms.append(m)
        for j, r in enumerate(rows):
            ol = _dot(ps[j], jnp.concatenate([vr_s[g, r, :].astype(BF16), ones], axis=1))
            if has_prev:
                ol = ol + _dot(pps[j], jnp.concatenate([vr_s[g, prows[j], :].astype(BF16), ones], axis=1))
            l = ol[:, HEAD_DIM:]
            og_s[g, r, :] = ol[:, :HEAD_DIM] * (1.0 / l)
            lg_s[g, r, :] = ms[j] + jnp.log(l)

    q_refs = (q0_ref, q1_ref, q2_ref)
    k_refs = (k0_ref, k1_ref, k2_ref)
    v_refs = (v0_ref, v1_ref, v2_ref)
    for g in range(len(DIL_GROUPS)):
        qr_s[g] = rope(q_refs[g][...].astype(F32))
        kr_s[g] = rope(k_refs[g][...].astype(F32))
        vr_s[g] = v_refs[g][...].astype(F32)
    for g, (window, dil) in enumerate(DIL_GROUPS):
        assert window // dil == blk
        nb = S // dil // blk
        block_group(g, list(range(dil)), dil, False)
        if nb > 1:
            block_group(g, [n * blk * dil + r for n in range(1, nb) for r in range(dil)], dil, True)

    l0, l1, l2 = lg_s[0], lg_s[1], lg_s[2]
    m = jnp.maximum(jnp.maximum(l0, l1), l2)
    e0, e1, e2 = jnp.exp(l0 - m), jnp.exp(l1 - m), jnp.exp(l2 - m)
    inv = 1.0 / (e0 + e1 + e2)
    o_ref[...] = ((e0 * inv) * og_s[0] + (e1 * inv) * og_s[1] + (e2 * inv) * og_s[2]).astype(o_ref.dtype)


def _dilated(P3, pos3, invf_row):
    B, S, _ = P3.shape

    def head_spec(base, g):
        return pl.BlockSpec((None, S, HEAD_DIM), lambda b, h: (b, 0, base // HEAD_DIM + g * DIL_HPG + h))

    ngr = len(DIL_GROUPS)
    in_specs = ([head_spec(COL_BQ, g) for g in range(ngr)] + [head_spec(COL_BK, g) for g in range(ngr)]
                + [head_spec(COL_BV, g) for g in range(ngr)]
                + [pl.BlockSpec((None, S, 1), lambda b, h: (b, 0, 0)),
                   pl.BlockSpec((1, HEAD_DIM), lambda b, h: (0, 0))])
    big = pltpu.VMEM((S, HEAD_DIM), F32)
    return pl.pallas_call(
        _dilated_kernel,
        grid=(B, DIL_HPG),
        in_specs=in_specs,
        out_specs=pl.BlockSpec((None, S, HEAD_DIM), lambda b, h: (b, 0, h)),
        out_shape=jax.ShapeDtypeStruct((B, S, DIL_OUT), BF16),
        scratch_shapes=[big, big] + [pltpu.VMEM((ngr, S, HEAD_DIM), F32)] * 5,
        compiler_params=_cparams(2),
        name="dilated",
    )(*([P3] * 9), pos3, invf_row)


def _mix_kernel(oa_ref, ob_ref, ga_ref, gb_ref, x_ref, wa_ref, wb_ref, wm_ref, g_ref, b_ref, o_ref):
    ya = _dot(oa_ref[...], wa_ref[...])
    yb = _dot(ob_ref[...], wb_ref[...])
    h = _sigmoid(ga_ref[...].astype(F32)) * ya + _sigmoid(gb_ref[...].astype(F32)) * yb
    mix = _dot(h.astype(BF16), wm_ref[...])
    o_ref[...] = _layer_norm(ALPHA * x_ref[...] + mix, g_ref[...], b_ref[...])


def _const_spec(shape):
    return pl.BlockSpec(shape, lambda *_: (0,) * len(shape), pipeline_mode=pl.Buffered(1))


def _mix(oa, ob, P, x2d, wa, wb, wm, g, b, tm):
    T = x2d.shape[0]
    row = lambda w, j=0: pl.BlockSpec((tm, w), lambda i: (i, j))
    return pl.pallas_call(
        _mix_kernel,
        grid=(T // tm,),
        in_specs=[row(DN_W), row(DIL_OUT), row(D_MODEL, COL_GA // D_MODEL), row(D_MODEL, COL_GB // D_MODEL),
                  row(D_MODEL), _const_spec(wa.shape), _const_spec(wb.shape), _const_spec(wm.shape),
                  _const_spec(g.shape), _const_spec(b.shape)],
        out_specs=row(D_MODEL),
        out_shape=jax.ShapeDtypeStruct((T, D_MODEL), F32),
        compiler_params=_cparams(1),
        name="mix",
    )(oa, ob, P, P, x2d, wa, wb, wm, g, b)


def _memkv_kernel(m_ref, w_ref, o_ref):
    o_ref[...] = _dot(m_ref[...].astype(BF16), w_ref[...]).astype(o_ref.dtype)


def _mem_kv(mem2d, wkv, tm):
    R = mem2d.shape[0]
    return pl.pallas_call(
        _memkv_kernel,
        grid=(R // tm,),
        in_specs=[pl.BlockSpec((tm, D_MODEL), lambda i: (i, 0)), _const_spec(wkv.shape)],
        out_specs=pl.BlockSpec((tm, 2 * D_MODEL), lambda i: (i, 0)),
        out_shape=jax.ShapeDtypeStruct((R, 2 * D_MODEL), BF16),
        compiler_params=_cparams(1),
        name="mem_kv",
    )(mem2d, wkv)


def _xattn_kernel(x_ref, kv_ref, wq_ref, wo_ref, g_ref, b_ref, o_ref):
    x = x_ref[...]
    q = _dot(x.astype(BF16), wq_ref[...])
    scale = MEM_HEAD_DIM ** -0.5
    outs = []
    for h in range(MEM_HEADS):
        cols = slice(h * MEM_HEAD_DIM, (h + 1) * MEM_HEAD_DIM)
        vcols = slice(D_MODEL + h * MEM_HEAD_DIM, D_MODEL + (h + 1) * MEM_HEAD_DIM)
        s = _dot_nt(q[:, cols].astype(BF16), kv_ref[:, cols]) * scale
        p = jnp.exp(s - jnp.max(s, axis=-1, keepdims=True))
        p = p * (1.0 / jnp.sum(p, axis=-1, keepdims=True))
        outs.append(_dot(p.astype(BF16), kv_ref[:, vcols]).astype(BF16))
    cross = _dot(jnp.concatenate(outs, axis=1), wo_ref[...])
    o_ref[...] = _layer_norm(ALPHA * x + cross, g_ref[...], b_ref[...])


def _xattn(x3, kv3, wq, wo, g, b, tm):
    B, S, _ = x3.shape
    M = kv3.shape[1]
    return pl.pallas_call(
        _xattn_kernel,
        grid=(B, S // tm),
        in_specs=[pl.BlockSpec((None, tm, D_MODEL), lambda bi, i: (bi, i, 0)),
                  pl.BlockSpec((None, M, 2 * D_MODEL), lambda bi, i: (bi, 0, 0)),
                  _const_spec(wq.shape), _const_spec(wo.shape), _const_spec(g.shape), _const_spec(b.shape)],
        out_specs=pl.BlockSpec((None, tm, D_MODEL), lambda bi, i: (bi, i, 0)),
        out_shape=jax.ShapeDtypeStruct((B, S, D_MODEL), F32),
        compiler_params=_cparams(2),
        name="xattn",
    )(x3, kv3, wq, wo, g, b)


def _ffn_kernel(x_ref, w13_ref, w2_ref, g_ref, b_ref, o_ref):
    x = x_ref[...]
    xb = x.astype(BF16)
    acc = jnp.zeros(x.shape, F32)
    for j in range(D_FF // FF_CHUNK):
        ca = slice(j * FF_CHUNK, (j + 1) * FF_CHUNK)
        cb = slice(D_FF + j * FF_CHUNK, D_FF + (j + 1) * FF_CHUNK)
        a = _dot(xb, w13_ref[:, ca])
        h = (a * _sigmoid(a)) * _dot(xb, w13_ref[:, cb])
        acc = acc + _dot(h.astype(BF16), w2_ref[ca, :])
    o_ref[...] = _layer_norm(ALPHA * x + acc, g_ref[...], b_ref[...])


def _ffn(x2d, w13, w2, g, b, tm):
    T = x2d.shape[0]
    return pl.pallas_call(
        _ffn_kernel,
        grid=(T // tm,),
        in_specs=[pl.BlockSpec((tm, D_MODEL), lambda i: (i, 0)),
                  _const_spec(w13.shape), _const_spec(w2.shape), _const_spec(g.shape), _const_spec(b.shape)],
        out_specs=pl.BlockSpec((tm, D_MODEL), lambda i: (i, 0)),
        out_shape=jax.ShapeDtypeStruct((T, D_MODEL), F32),
        compiler_params=_cparams(1),
        name="ffn",
    )(x2d, w13, w2, g, b)


def _relayout_w_in(w):
    sizes = (3 * DN_W, DN_W, DN_HEADS, DN_HEADS, DIL_W, DIL_W, DIL_W, D_MODEL, D_MODEL)
    offs = [0]
    for s in sizes:
        offs.append(offs[-1] + s)
    part = lambda i: w[:, offs[i]:offs[i + 1]]
    used = COL_GATES + 2 * DN_HEADS
    pad = jnp.zeros((w.shape[0], NP - used), w.dtype)
    return jnp.concatenate([part(0), part(1), part(7), part(8), part(4), part(5), part(6),
                            part(2), part(3), pad], axis=1)


def _pad_row(v, fill=0.0):
    return jnp.concatenate([v.astype(F32), jnp.full((LANES - v.shape[0],), fill, F32)])[None, :]


def kernel(x, mem, positions, w_in, conv_w, a_log, dt_bias, dn_norm_g, w_br_a, w_br_b, w_mix_out,
           ln1_g, ln1_b, w_xq, w_xkv, w_xo, ln2_g, ln2_b, w_ffn13, w_ffn2, ln3_g, ln3_b):
    B, S, D = x.shape
    T = B * S
    M = mem.shape[1]
    assert D == D_MODEL and w_in.shape[0] == 1 and S % (DIL_BLOCK * DIL_GROUPS[-1][1]) == 0
    x2d = x.reshape(T, D)
    bf = lambda w: w[0].astype(BF16)
    row = lambda v: v[0][None, :].astype(F32)

    P, G = _in_proj(x2d, _relayout_w_in(w_in[0]).astype(BF16), tm=S)
    P3 = P.reshape(B, S, NP)

    o_a = _deltanet(P3, G.reshape(B, S, LANES), conv_w[0], _pad_row(a_log[0]), _pad_row(dt_bias[0]), row(dn_norm_g))

    inv_freq = ROPE_THETA ** (-jnp.arange(0, HEAD_DIM, 2, dtype=F32) / HEAD_DIM)
    invf_row = jnp.concatenate([inv_freq, inv_freq])[None, :]
    o_b = _dilated(P3, positions.reshape(B, S, 1), invf_row)

    x1 = _mix(o_a.reshape(T, DN_W), o_b.reshape(T, DIL_OUT), P, x2d, bf(w_br_a), bf(w_br_b), bf(w_mix_out),
              row(ln1_g), row(ln1_b), tm=512)

    kv = _mem_kv(mem.reshape(B * M, D), bf(w_xkv), tm=512)
    x2 = _xattn(x1.reshape(B, S, D), kv.reshape(B, M, 2 * D), bf(w_xq), bf(w_xo), row(ln2_g), row(ln2_b), tm=512)

    out = _ffn(x2.reshape(T, D), bf(w_ffn13), bf(w_ffn2), row(ln3_g), row(ln3_b), tm=512)
    return out.reshape(B, S, D)
```

```python
import functools
import math

import jax
import jax.numpy as jnp
from jax import lax
from jax.experimental import pallas as pl
from jax.experimental.pallas import tpu as pltpu

F32 = jnp.float32
BF16 = jnp.bfloat16

LANES = 128
D_MODEL = 1024
DN_HEADS = 8
HEAD_DIM = 128
DN_W = DN_HEADS * HEAD_DIM
DN_CONV = 4
DN_BLOCK = 128
DN_HPG = 2
INV_BASE = 16
DN_PARTS = 4
DIL_GROUPS = ((128, 1), (512, 4), (2048, 16))
DIL_HPG = 4
DIL_W = len(DIL_GROUPS) * DIL_HPG * HEAD_DIM
DIL_OUT = DIL_HPG * HEAD_DIM
DIL_BLOCK = 128
ROPE_THETA = 10000.0
MEM_HEADS = 4
MEM_HEAD_DIM = D_MODEL // MEM_HEADS
D_FF = 2816
FF_CHUNK = 256
ALPHA = 2.0 ** 0.25
LN_EPS = 1e-5
RMS_EPS = 1e-6
LOG2_E = 1.4426950408889634

COL_QKV = 0
COL_Z = 3 * DN_W
COL_GA = COL_Z + DN_W
COL_GB = COL_GA + D_MODEL
COL_BQ = COL_GB + D_MODEL
COL_BK = COL_BQ + DIL_W
COL_BV = COL_BK + DIL_W
COL_GATES = COL_BV + DIL_W
PROJ_TN = 1024
PROJ_CONV_ROWS = 128
NP = 11264

VMEM_LIMIT = 52 * 1024 * 1024


def _cparams(n_axes):
    return pltpu.CompilerParams(dimension_semantics=("arbitrary",) * n_axes,
                                vmem_limit_bytes=VMEM_LIMIT)


def _dot(a, b):
    return jnp.dot(a, b, preferred_element_type=F32)


def _dot_nt(a, b):
    return lax.dot_general(a, b, (((1,), (1,)), ((), ())), preferred_element_type=F32)


def _sigmoid(x):
    return 1.0 / (1.0 + jnp.exp2(x * -LOG2_E))


def _silu(x):
    return x * _sigmoid(x)


def _layer_norm(y, g, b):
    mu = jnp.mean(y, axis=-1, keepdims=True)
    yc = y - mu
    var = jnp.mean(yc * yc, axis=-1, keepdims=True)
    return yc * lax.rsqrt(var + LN_EPS) * g + b


def _proj_kernel(x_ref, w_ref, cw_ref, o_ref, gates_ref, xb_s):
    S = x_ref.shape[0]
    j = pl.program_id(1)
    n_conv = 3 * DN_W // PROJ_TN

    @pl.when(j == 0)
    def _():
        xb_s[...] = x_ref[...].astype(BF16)

    @pl.when(j < n_conv)
    def _():
        halo = 8
        w = w_ref[...]
        cw = cw_ref[...]
        tail = jnp.zeros((halo, PROJ_TN), F32)
        n_chunks = S // PROJ_CONV_ROWS
        chunk = lambda c: _dot(xb_s[c * PROJ_CONV_ROWS:(c + 1) * PROJ_CONV_ROWS, :], w)
        nxt = chunk(0)
        for c in range(n_chunks):
            rows = slice(c * PROJ_CONV_ROWS, (c + 1) * PROJ_CONV_ROWS)
            acc = nxt
            if c + 1 < n_chunks:
                nxt = chunk(c + 1)
            win = jnp.concatenate([tail, acc], axis=0)
            tail = acc[PROJ_CONV_ROWS - halo:, :]
            y = acc * cw[DN_CONV - 1:DN_CONV, :]
            for t in range(DN_CONV - 1):
                y = y + pltpu.roll(win, DN_CONV - 1 - t, 0)[halo:, :] * cw[t:t + 1, :]
            o_ref[rows, :] = _silu(y).astype(o_ref.dtype)

    @pl.when(j == COL_Z // PROJ_TN)
    def _():
        w = w_ref[...]
        n_chunks = S // PROJ_CONV_ROWS
        chunk = lambda c: _dot(xb_s[c * PROJ_CONV_ROWS:(c + 1) * PROJ_CONV_ROWS, :], w)
        nxt = chunk(0)
        for c in range(n_chunks):
            acc = nxt
            if c + 1 < n_chunks:
                nxt = chunk(c + 1)
            o_ref[c * PROJ_CONV_ROWS:(c + 1) * PROJ_CONV_ROWS, :] = _silu(acc).astype(o_ref.dtype)

    @pl.when(j > COL_Z // PROJ_TN)
    def _():
        acc = _dot(xb_s[...], w_ref[...])
        o_ref[...] = acc.astype(o_ref.dtype)

        @pl.when(j == COL_GATES // PROJ_TN)
        def _():
            lo = COL_GATES % PROJ_TN
            gates_ref[...] = acc[:, lo:lo + LANES]


def _in_proj(x2d, wb, conv_w, tm):
    T = x2d.shape[0]
    n_conv = 3 * DN_W // PROJ_TN
    assert COL_QKV == 0 and COL_Z == 3 * DN_W and DN_W % PROJ_TN == 0
    return pl.pallas_call(
        _proj_kernel,
        grid=(T // tm, NP // PROJ_TN),
        in_specs=[pl.BlockSpec((tm, D_MODEL), lambda i, j: (i, 0)),
                  pl.BlockSpec((D_MODEL, PROJ_TN), lambda i, j: (0, j)),
                  pl.BlockSpec((DN_CONV, PROJ_TN), lambda i, j: (0, jnp.minimum(j, n_conv - 1)))],
        out_specs=[pl.BlockSpec((tm, PROJ_TN), lambda i, j: (i, j)),
                   pl.BlockSpec((tm, LANES), lambda i, j: (i, 0))],
        out_shape=[jax.ShapeDtypeStruct((T, NP), BF16), jax.ShapeDtypeStruct((T, LANES), F32)],
        scratch_shapes=[pltpu.VMEM((tm, D_MODEL), BF16)],
        compiler_params=_cparams(2),
        name="in_proj",
    )(x2d, wb, conv_w)


def _l2norm(x):
    return x * lax.rsqrt(jnp.sum(x * x, axis=-1, keepdims=True) + RMS_EPS)


def _deltanet_kernel(q_ref, k_ref, v_ref, z_ref, gates_ref, alog_ref, dtb_ref, ng_ref, o_ref,
                     gam_s, beta_s, n_s, t_s, p_s, qk_s, rhs_s, mq_s, kdt_s, qd_s, gt_s):
    S = q_ref.shape[0]
    nblk = S // DN_BLOCK
    hg = pl.program_id(1)
    blk = DN_BLOCK
    bc_s, o0_s = n_s, t_s

    @pl.when(hg == 0)
    def _():
        G = gates_ref[...]
        zz = G + dtb_ref[...]
        softplus = jnp.maximum(zz, 0.0) + jnp.log1p(jnp.exp(-jnp.abs(zz)))
        gam = -jnp.exp(alog_ref[...]) * softplus
        rowc = lax.broadcasted_iota(jnp.int32, G.shape, 0) & (blk - 1)
        sh = 1
        while sh < blk:
            gam = gam + jnp.where(rowc >= sh, pltpu.roll(gam, sh, 0), 0.0)
            sh *= 2
        gam_s[...] = gam
        beta_s[...] = _sigmoid(G)

    ri = lax.broadcasted_iota(jnp.int32, (blk, blk), 0)
    ci = lax.broadcasted_iota(jnp.int32, (blk, blk), 1)

    def same_block(b):
        s = int(math.log2(b))
        return (ri >> s) == (ci >> s)

    def setup(c, carry=None):
        rows = pl.ds(pl.multiple_of(c * blk, blk), blk)
        gsl = gam_s[rows, :]
        bsl = beta_s[rows, :]
        for i in range(DN_HPG):
            h = hg * DN_HPG + i
            cols = slice(i * HEAD_DIM, (i + 1) * HEAD_DIM)
            gam_col = jnp.sum(jnp.where(ci == h, gsl, 0.0), axis=1, keepdims=True)
            beta_col = jnp.sum(jnp.where(ci == h + DN_HEADS, bsl, 0.0), axis=1, keepdims=True)
            gb = jnp.broadcast_to(gam_col, (blk, HEAD_DIM))
            kn = _l2norm(k_ref[rows, cols].astype(F32))
            qn = _l2norm(q_ref[rows, cols].astype(F32)) * (HEAD_DIM ** -0.5)
            kb = kn * beta_col
            dec = jnp.exp(jnp.where(ri >= ci, gb - gb.T, -jnp.inf))
            a = _dot_nt(jnp.concatenate([qn, kb], axis=0).astype(BF16), kn.astype(BF16))
            qk_s[i, c] = (a[:blk] * dec).astype(BF16)
            n_s[i, c] = -a[blk:] * jnp.where(ri > ci, dec, 0.0)
            eg = jnp.exp(gb)
            rhs_s[i, c] = jnp.concatenate([v_ref[rows, cols].astype(F32) * beta_col, kb * eg], axis=1).astype(BF16)
            qd_s[i, c] = qn * eg
            gl = jnp.broadcast_to(gb[blk - 1:blk, :], (blk, HEAD_DIM))
            kdt_s[i, c] = (kn * jnp.exp(gl - gb)).T.astype(BF16)
            gt_s[i, c] = jnp.exp(gl[:8, :])
        return carry

    def series_start(i, c):
        d = jnp.where(same_block(INV_BASE), n_s[i, c], 0.0)
        return d, _dot(d.astype(BF16), d.astype(BF16))

    def series_start_done(i, c, r):
        t_s[i, c] = (ri == ci).astype(F32) + r[0]
        p_s[i, c] = r[1].astype(BF16)

    def series_both(i, c):
        pb = p_s[i, c]
        return _dot(jnp.concatenate([t_s[i, c].astype(BF16), pb], axis=0), pb)

    def series_both_done(i, c, r):
        t_s[i, c] = t_s[i, c] + r[:blk]
        p_s[i, c] = r[blk:].astype(BF16)

    def times_p(i, c):
        return _dot(t_s[i, c].astype(BF16), p_s[i, c])

    def add_to_t(i, c, r):
        t_s[i, c] = t_s[i, c] + r

    def to_p(i, c, r):
        p_s[i, c] = r.astype(BF16)

    stages = [(series_start, series_start_done)]
    stages += [(series_both, series_both_done)] * (int(math.log2(INV_BASE)) - 2)
    stages += [(times_p, add_to_t)]
    b = INV_BASE
    while b < blk:
        off = same_block(2 * b) & jnp.logical_not(same_block(b))

        def off_times_t(i, c, off=off):
            return _dot(jnp.where(off, n_s[i, c], 0.0).astype(BF16), t_s[i, c].astype(BF16))

        stages += [(off_times_t, to_p), (times_p, add_to_t)]
        b *= 2

    def solve(i, c):
        return _dot(t_s[i, c].astype(BF16), rhs_s[i, c])

    def solve_done(i, c, r):
        rhs_s[i, c] = r.astype(BF16)

    def cross(i, c):
        return _dot(jnp.concatenate([kdt_s[i, c], qk_s[i, c]], axis=0), rhs_s[i, c])

    def cross_done(i, c, r):
        bc_s[i, c] = r[:blk, :HEAD_DIM]
        o0_s[i, c] = r[blk:, :HEAD_DIM]
        mq_s[i, c] = jnp.concatenate([r[:blk, HEAD_DIM:], qd_s[i, c] - r[blk:, HEAD_DIM:]], axis=0).astype(BF16)

    stages += [(solve, solve_done), (cross, cross_done)]

    def run_stages(blocks, side_work):
        probs = [(i, c) for c in blocks for i in range(DN_HPG)]
        for k, (issue, finish) in enumerate(stages):
            results = [issue(i, c) for i, c in probs]
            if k < len(side_work):
                side_work[k]()
            for (i, c), r in zip(probs, results):
                finish(i, c, r)

    ng = ng_ref[...]

    def scan_step(c, states):
        rows = pl.ds(pl.multiple_of(c * blk, blk), blk)
        new_states = []
        for i in range(DN_HPG):
            cols = slice(i * HEAD_DIM, (i + 1) * HEAD_DIM)
            st = states[i]
            r = _dot(mq_s[i, c], st.astype(BF16))
            g_tot = jnp.broadcast_to(gt_s[i, c][0:1, :], (HEAD_DIM, HEAD_DIM))
            new_states.append(g_tot * st + bc_s[i, c] - r[:blk])
            o = o0_s[i, c] + r[blk:]
            o = o * lax.rsqrt(jnp.mean(o * o, axis=-1, keepdims=True) + RMS_EPS) * ng
            o_ref[rows, cols] = (o * z_ref[rows, cols].astype(F32)).astype(o_ref.dtype)
        return tuple(new_states)

    per = nblk // DN_PARTS
    parts = [list(range(p * per, (p + 1) * per)) for p in range(DN_PARTS)]
    lax.fori_loop(0, per, setup, 0)

    state = [tuple(jnp.zeros((HEAD_DIM, HEAD_DIM), F32) for _ in range(DN_HPG))]

    def scan_side(c):
        state[0] = scan_step(c, state[0])

    for p in range(DN_PARTS):
        side = []
        for j in range(per):
            if p + 1 < DN_PARTS:
                side.append(functools.partial(setup, parts[p + 1][j]))
            if p > 0:
                side.append(functools.partial(scan_side, parts[p - 1][j]))
        run_stages(parts[p], side)
    lax.fori_loop(nblk - per, nblk, scan_step, state[0])


def _deltanet(P3, G3, alog_row, dtb_row, ng_row):
    B, S, _ = P3.shape
    W = DN_HPG * HEAD_DIM
    ngrp = DN_HEADS // DN_HPG
    nblk = S // DN_BLOCK

    def col_spec(base):
        return pl.BlockSpec((None, S, W), lambda b, g: (b, 0, base // W + g))

    row_spec = pl.BlockSpec((1, LANES), lambda b, g: (0, 0))
    blk_buf = lambda rows, cols, dt: pltpu.VMEM((DN_HPG, nblk, rows, cols), dt)
    return pl.pallas_call(
        _deltanet_kernel,
        grid=(B, ngrp),
        in_specs=[col_spec(COL_QKV), col_spec(COL_QKV + DN_W), col_spec(COL_QKV + 2 * DN_W), col_spec(COL_Z),
                  pl.BlockSpec((None, S, LANES), lambda b, g: (b, 0, 0)),
                  row_spec, row_spec, row_spec],
        out_specs=pl.BlockSpec((None, S, W), lambda b, g: (b, 0, g)),
        out_shape=jax.ShapeDtypeStruct((B, S, DN_W), BF16),
        scratch_shapes=[pltpu.VMEM((S, LANES), F32), pltpu.VMEM((S, LANES), F32),
                        blk_buf(DN_BLOCK, DN_BLOCK, F32), blk_buf(DN_BLOCK, DN_BLOCK, F32),
                        blk_buf(DN_BLOCK, DN_BLOCK, BF16), blk_buf(DN_BLOCK, DN_BLOCK, BF16),
                        blk_buf(DN_BLOCK, 2 * HEAD_DIM, BF16),
                        blk_buf(2 * DN_BLOCK, HEAD_DIM, BF16),
                        blk_buf(HEAD_DIM, DN_BLOCK, BF16),
                        blk_buf(DN_BLOCK, HEAD_DIM, F32),
                        blk_buf(8, HEAD_DIM, F32)],
        compiler_params=_cparams(2),
        name="deltanet",
    )(P3, P3, P3, P3, G3, alog_row, dtb_row, ng_row)


def _dilated_kernel(q0_ref, q1_ref, q2_ref, k0_ref, k1_ref, k2_ref, v0_ref, v1_ref, v2_ref,
                    pos_ref, invf_ref, o_ref, cos_s, sin_s, qr_s, kr_s, vr_s, og_s, lg_s):
    S = q0_ref.shape[0]
    blk = DIL_BLOCK
    scale = HEAD_DIM ** -0.5

    @pl.when(pl.program_id(1) == 0)
    def _():
        ang = pos_ref[...].astype(F32) * invf_ref[...]
        lane = lax.broadcasted_iota(jnp.int32, ang.shape, 1)
        cos_s[...] = jnp.cos(ang)
        sin_s[...] = jnp.where(lane < HEAD_DIM // 2, -jnp.sin(ang), jnp.sin(ang))

    def rope(x):
        return x * cos_s[...] + pltpu.roll(x, HEAD_DIM // 2, 1) * sin_s[...]

    ri = lax.broadcasted_iota(jnp.int32, (blk, blk), 0)
    ci = lax.broadcasted_iota(jnp.int32, (blk, blk), 1)
    neg = -jnp.inf

    def rows_of(start, dil):
        return pl.ds(start, blk) if dil == 1 else pl.ds(start, blk, stride=dil)

    ones = jnp.ones((blk, HEAD_DIM), BF16)

    def block_group(g, starts, dil, has_prev):
        rows = [rows_of(st, dil) for st in starts]
        q = [qr_s[g, r, :].astype(BF16) for r in rows]
        s = [_dot_nt(q[j], kr_s[g, r, :].astype(BF16)) for j, r in enumerate(rows)]
        if has_prev:
            prows = [rows_of(st - blk * dil, dil) for st in starts]
            sp = [_dot_nt(q[j], kr_s[g, r, :].astype(BF16)) for j, r in enumerate(prows)]
        ps, pps, ms = [], [], []
        for j in range(len(starts)):
            sj = jnp.where(ri >= ci, s[j] * scale, neg)
            if has_prev:
                spj = jnp.where(ci >= ri, sp[j] * scale, neg)
                m = jnp.max(jnp.maximum(sj, spj), axis=-1, keepdims=True)
                pps.append(jnp.exp(spj - m).astype(BF16))
            else:
                m = jnp.max(sj, axis=-1, keepdims=True)
            ps.append(jnp.exp(sj - m).astype(BF16))
            ms.append(m)
        for j, r in enumerate(rows):
            ol = _dot(ps[j], jnp.concatenate([vr_s[g, r, :].astype(BF16), ones], axis=1))
            if has_prev:
                ol = ol + _dot(pps[j], jnp.concatenate([vr_s[g, prows[j], :].astype(BF16), ones], axis=1))
            l = ol[:, HEAD_DIM:]
            og_s[g, r, :] = ol[:, :HEAD_DIM] * (1.0 / l)
            lg_s[g, r, :] = ms[j] + jnp.log(l)

    q_refs = (q0_ref, q1_ref, q2_ref)
    k_refs = (k0_ref, k1_ref, k2_ref)
    v_refs = (v0_ref, v1_ref, v2_ref)
    for g in range(len(DIL_GROUPS)):
        qr_s[g] = rope(q_refs[g][...].astype(F32))
        kr_s[g] = rope(k_refs[g][...].astype(F32))
        vr_s[g] = v_refs[g][...].astype(F32)
    for g, (window, dil) in enumerate(DIL_GROUPS):
        assert window // dil == blk
        nb = S // dil // blk
        block_group(g, list(range(dil)), dil, False)
        if nb > 1:
            block_group(g, [n * blk * dil + r for n in range(1, nb) for r in range(dil)], dil, True)

    l0, l1, l2 = lg_s[0], lg_s[1], lg_s[2]
    m = jnp.maximum(jnp.maximum(l0, l1), l2)
    e0, e1, e2 = jnp.exp(l0 - m), jnp.exp(l1 - m), jnp.exp(l2 - m)
    inv = 1.0 / (e0 + e1 + e2)
    o_ref[...] = ((e0 * inv) * og_s[0] + (e1 * inv) * og_s[1] + (e2 * inv) * og_s[2]).astype(o_ref.dtype)


def _dilated(P3, pos3, invf_row):
    B, S, _ = P3.shape

    def head_spec(base, g):
        return pl.BlockSpec((None, S, HEAD_DIM), lambda b, h: (b, 0, base // HEAD_DIM + g * DIL_HPG + h))

    ngr = len(DIL_GROUPS)
    in_specs = ([head_spec(COL_BQ, g) for g in range(ngr)] + [head_spec(COL_BK, g) for g in range(ngr)]
                + [head_spec(COL_BV, g) for g in range(ngr)]
                + [pl.BlockSpec((None, S, 1), lambda b, h: (b, 0, 0)),
                   pl.BlockSpec((1, HEAD_DIM), lambda b, h: (0, 0))])
    big = pltpu.VMEM((S, HEAD_DIM), F32)
    return pl.pallas_call(
        _dilated_kernel,
        grid=(B, DIL_HPG),
        in_specs=in_specs,
        out_specs=pl.BlockSpec((None, S, HEAD_DIM), lambda b, h: (b, 0, h)),
        out_shape=jax.ShapeDtypeStruct((B, S, DIL_OUT), BF16),
        scratch_shapes=[big, big] + [pltpu.VMEM((ngr, S, HEAD_DIM), F32)] * 5,
        compiler_params=_cparams(2),
        name="dilated",
    )(*([P3] * 9), pos3, invf_row)


def _mix_kernel(oa_ref, ob_ref, ga_ref, gb_ref, x_ref, wa_ref, wb_ref, wm_ref, g_ref, b_ref, o_ref):
    ya = _dot(oa_ref[...], wa_ref[...])
    yb = _dot(ob_ref[...], wb_ref[...])
    h = _sigmoid(ga_ref[...].astype(F32)) * ya + _sigmoid(gb_ref[...].astype(F32)) * yb
    mix = _dot(h.astype(BF16), wm_ref[...])
    o_ref[...] = _layer_norm(ALPHA * x_ref[...] + mix, g_ref[...], b_ref[...])


def _const_spec(shape):
    return pl.BlockSpec(shape, lambda *_: (0,) * len(shape), pipeline_mode=pl.Buffered(1))


def _mix(oa, ob, P, x2d, wa, wb, wm, g, b, tm):
    T = x2d.shape[0]
    row = lambda w, j=0: pl.BlockSpec((tm, w), lambda i: (i, j))
    return pl.pallas_call(
        _mix_kernel,
        grid=(T // tm,),
        in_specs=[row(DN_W), row(DIL_OUT), row(D_MODEL, COL_GA // D_MODEL), row(D_MODEL, COL_GB // D_MODEL),
                  row(D_MODEL), _const_spec(wa.shape), _const_spec(wb.shape), _const_spec(wm.shape),
                  _const_spec(g.shape), _const_spec(b.shape)],
        out_specs=row(D_MODEL),
        out_shape=jax.ShapeDtypeStruct((T, D_MODEL), F32),
        compiler_params=_cparams(1),
        name="mix",
    )(oa, ob, P, P, x2d, wa, wb, wm, g, b)


def _memkv_kernel(m_ref, w_ref, o_ref):
    o_ref[...] = _dot(m_ref[...].astype(BF16), w_ref[...]).astype(o_ref.dtype)


def _mem_kv(mem2d, wkv, tm):
    R = mem2d.shape[0]
    return pl.pallas_call(
        _memkv_kernel,
        grid=(R // tm,),
        in_specs=[pl.BlockSpec((tm, D_MODEL), lambda i: (i, 0)), _const_spec(wkv.shape)],
        out_specs=pl.BlockSpec((tm, 2 * D_MODEL), lambda i: (i, 0)),
        out_shape=jax.ShapeDtypeStruct((R, 2 * D_MODEL), BF16),
        compiler_params=_cparams(1),
        name="mem_kv",
    )(mem2d, wkv)


def _xattn_kernel(x_ref, kv_ref, wq_ref, wo_ref, g_ref, b_ref, o_ref):
    x = x_ref[...]
    q = _dot(x.astype(BF16), wq_ref[...])
    scale = MEM_HEAD_DIM ** -0.5
    outs = []
    for h in range(MEM_HEADS):
        cols = slice(h * MEM_HEAD_DIM, (h + 1) * MEM_HEAD_DIM)
        vcols = slice(D_MODEL + h * MEM_HEAD_DIM, D_MODEL + (h + 1) * MEM_HEAD_DIM)
        s = _dot_nt(q[:, cols].astype(BF16), kv_ref[:, cols]) * scale
        p = jnp.exp(s - jnp.max(s, axis=-1, keepdims=True))
        p = p * (1.0 / jnp.sum(p, axis=-1, keepdims=True))
        outs.append(_dot(p.astype(BF16), kv_ref[:, vcols]).astype(BF16))
    cross = _dot(jnp.concatenate(outs, axis=1), wo_ref[...])
    o_ref[...] = _layer_norm(ALPHA * x + cross, g_ref[...], b_ref[...])


def _xattn(x3, kv3, wq, wo, g, b, tm):
    B, S, _ = x3.shape
    M = kv3.shape[1]
    return pl.pallas_call(
        _xattn_kernel,
        grid=(B, S // tm),
        in_specs=[pl.BlockSpec((None, tm, D_MODEL), lambda bi, i: (bi, i, 0)),
                  pl.BlockSpec((None, M, 2 * D_MODEL), lambda bi, i: (bi, 0, 0)),
                  _const_spec(wq.shape), _const_spec(wo.shape), _const_spec(g.shape), _const_spec(b.shape)],
        out_specs=pl.BlockSpec((None, tm, D_MODEL), lambda bi, i: (bi, i, 0)),
        out_shape=jax.ShapeDtypeStruct((B, S, D_MODEL), F32),
        compiler_params=_cparams(2),
        name="xattn",
    )(x3, kv3, wq, wo, g, b)


def _ffn_kernel(x_ref, w13_ref, w2_ref, g_ref, b_ref, o_ref):
    x = x_ref[...]
    xb = x.astype(BF16)
    acc = jnp.zeros(x.shape, F32)
    for j in range(D_FF // FF_CHUNK):
        ca = slice(j * FF_CHUNK, (j + 1) * FF_CHUNK)
        cb = slice(D_FF + j * FF_CHUNK, D_FF + (j + 1) * FF_CHUNK)
        a = _dot(xb, w13_ref[:, ca])
        h = (a * _sigmoid(a)) * _dot(xb, w13_ref[:, cb])
        acc = acc + _dot(h.astype(BF16), w2_ref[ca, :])
    o_ref[...] = _layer_norm(ALPHA * x + acc, g_ref[...], b_ref[...])


def _ffn(x2d, w13, w2, g, b, tm):
    T = x2d.shape[0]
    return pl.pallas_call(
        _ffn_kernel,
        grid=(T // tm,),
        in_specs=[pl.BlockSpec((tm, D_MODEL), lambda i: (i, 0)),
                  _const_spec(w13.shape), _const_spec(w2.shape), _const_spec(g.shape), _const_spec(b.shape)],
        out_specs=pl.BlockSpec((tm, D_MODEL), lambda i: (i, 0)),
        out_shape=jax.ShapeDtypeStruct((T, D_MODEL), F32),
        compiler_params=_cparams(1),
        name="ffn",
    )(x2d, w13, w2, g, b)


def _relayout_w_in(w):
    sizes = (3 * DN_W, DN_W, DN_HEADS, DN_HEADS, DIL_W, DIL_W, DIL_W, D_MODEL, D_MODEL)
    offs = [0]
    for s in sizes:
        offs.append(offs[-1] + s)
    part = lambda i: w[:, offs[i]:offs[i + 1]]
    used = COL_GATES + 2 * DN_HEADS
    pad = jnp.zeros((w.shape[0], NP - used), w.dtype)
    return jnp.concatenate([part(0), part(1), part(7), part(8), part(4), part(5), part(6),
                            part(2), part(3), pad], axis=1)


def _pad_row(v, fill=0.0):
    return jnp.concatenate([v.astype(F32), jnp.full((LANES - v.shape[0],), fill, F32)])[None, :]


def kernel(x, mem, positions, w_in, conv_w, a_log, dt_bias, dn_norm_g, w_br_a, w_br_b, w_mix_out,
           ln1_g, ln1_b, w_xq, w_xkv, w_xo, ln2_g, ln2_b, w_ffn13, w_ffn2, ln3_g, ln3_b):
    B, S, D = x.shape
    T = B * S
    M = mem.shape[1]
    assert D == D_MODEL and w_in.shape[0] == 1 and S % (DIL_BLOCK * DIL_GROUPS[-1][1]) == 0
    x2d = x.reshape(T, D)
    bf = lambda w: w[0].astype(BF16)
    row = lambda v: v[0][None, :].astype(F32)

    P, G = _in_proj(x2d, _relayout_w_in(w_in[0]).astype(BF16), conv_w[0], tm=S)
    P3 = P.reshape(B, S, NP)

    o_a = _deltanet(P3, G.reshape(B, S, LANES), _pad_row(a_log[0]), _pad_row(dt_bias[0]), row(dn_norm_g))

    inv_freq = ROPE_THETA ** (-jnp.arange(0, HEAD_DIM, 2, dtype=F32) / HEAD_DIM)
    invf_row = jnp.concatenate([inv_freq, inv_freq])[None, :]
    o_b = _dilated(P3, positions.reshape(B, S, 1), invf_row)

    x1 = _mix(o_a.reshape(T, DN_W), o_b.reshape(T, DIL_OUT), P, x2d, bf(w_br_a), bf(w_br_b), bf(w_mix_out),
              row(ln1_g), row(ln1_b), tm=512)

    kv = _mem_kv(mem.reshape(B * M, D), bf(w_xkv), tm=512)
    x2 = _xattn(x1.reshape(B, S, D), kv.reshape(B, M, 2 * D), bf(w_xq), bf(w_xo), row(ln2_g), row(ln2_b), tm=512)

    out = _ffn(x2.reshape(T, D), bf(w_ffn13), bf(w_ffn2), row(ln3_g), row(ln3_b), tm=512)
    return out.reshape(B, S, D)
```

```python
import functools
import math

import jax
import jax.numpy as jnp
from jax import lax
from jax.experimental import pallas as pl
from jax.experimental.pallas import tpu as pltpu

F32 = jnp.float32
BF16 = jnp.bfloat16

LANES = 128
D_MODEL = 1024
DN_HEADS = 8
HEAD_DIM = 128
DN_W = DN_HEADS * HEAD_DIM
DN_CONV = 4
DN_BLOCK = 128
DN_HPG = 2
INV_BASE = 16
DN_PARTS = 4
DIL_GROUPS = ((128, 1), (512, 4), (2048, 16))
DIL_HPG = 4
DIL_W = len(DIL_GROUPS) * DIL_HPG * HEAD_DIM
DIL_OUT = DIL_HPG * HEAD_DIM
DIL_BLOCK = 128
ROPE_THETA = 10000.0
MEM_HEADS = 4
MEM_HEAD_DIM = D_MODEL // MEM_HEADS
D_FF = 2816
FF_CHUNK = 256
ROW_SUB = 512
ALPHA = 2.0 ** 0.25
LN_EPS = 1e-5
RMS_EPS = 1e-6
LOG2_E = 1.4426950408889634

COL_QKV = 0
COL_Z = 3 * DN_W
COL_GA = COL_Z + DN_W
COL_GB = COL_GA + D_MODEL
COL_BQ = COL_GB + D_MODEL
COL_BK = COL_BQ + DIL_W
COL_BV = COL_BK + DIL_W
COL_GATES = COL_BV + DIL_W
PROJ_TN = 1024
PROJ_CONV_ROWS = 128
NP = 11264

VMEM_LIMIT = 52 * 1024 * 1024


def _cparams(n_axes):
    return pltpu.CompilerParams(dimension_semantics=("arbitrary",) * n_axes,
                                vmem_limit_bytes=VMEM_LIMIT)


def _dot(a, b):
    return jnp.dot(a, b, preferred_element_type=F32)


def _dot_nt(a, b):
    return lax.dot_general(a, b, (((1,), (1,)), ((), ())), preferred_element_type=F32)


def _sigmoid(x):
    return 1.0 / (1.0 + jnp.exp2(x * -LOG2_E))


def _silu(x):
    return x * _sigmoid(x)


def _layer_norm(y, g, b):
    mu = jnp.mean(y, axis=-1, keepdims=True)
    yc = y - mu
    var = jnp.mean(yc * yc, axis=-1, keepdims=True)
    return yc * lax.rsqrt(var + LN_EPS) * g + b


def _proj_kernel(x_ref, w_ref, cw_ref, o_ref, gates_ref, xb_s):
    S = x_ref.shape[0]
    j = pl.program_id(1)
    n_conv = 3 * DN_W // PROJ_TN

    @pl.when(j == 0)
    def _():
        xb_s[...] = x_ref[...].astype(BF16)

    @pl.when(j < n_conv)
    def _():
        halo = 8
        w = w_ref[...]
        cw = cw_ref[...]
        tail = jnp.zeros((halo, PROJ_TN), F32)
        n_chunks = S // PROJ_CONV_ROWS
        chunk = lambda c: _dot(xb_s[c * PROJ_CONV_ROWS:(c + 1) * PROJ_CONV_ROWS, :], w)
        nxt = chunk(0)
        for c in range(n_chunks):
            rows = slice(c * PROJ_CONV_ROWS, (c + 1) * PROJ_CONV_ROWS)
            acc = nxt
            if c + 1 < n_chunks:
                nxt = chunk(c + 1)
            win = jnp.concatenate([tail, acc], axis=0)
            tail = acc[PROJ_CONV_ROWS - halo:, :]
            y = acc * cw[DN_CONV - 1:DN_CONV, :]
            for t in range(DN_CONV - 1):
                y = y + pltpu.roll(win, DN_CONV - 1 - t, 0)[halo:, :] * cw[t:t + 1, :]
            o_ref[rows, :] = _silu(y).astype(o_ref.dtype)

    @pl.when(j == COL_Z // PROJ_TN)
    def _():
        w = w_ref[...]
        n_chunks = S // PROJ_CONV_ROWS
        chunk = lambda c: _dot(xb_s[c * PROJ_CONV_ROWS:(c + 1) * PROJ_CONV_ROWS, :], w)
        nxt = chunk(0)
        for c in range(n_chunks):
            acc = nxt
            if c + 1 < n_chunks:
                nxt = chunk(c + 1)
            o_ref[c * PROJ_CONV_ROWS:(c + 1) * PROJ_CONV_ROWS, :] = _silu(acc).astype(o_ref.dtype)

    @pl.when(j > COL_Z // PROJ_TN)
    def _():
        acc = _dot(xb_s[...], w_ref[...])
        o_ref[...] = acc.astype(o_ref.dtype)

        @pl.when(j == COL_GATES // PROJ_TN)
        def _():
            lo = COL_GATES % PROJ_TN
            gates_ref[...] = acc[:, lo:lo + LANES]


def _in_proj(x2d, wb, conv_w, tm):
    T = x2d.shape[0]
    n_conv = 3 * DN_W // PROJ_TN
    assert COL_QKV == 0 and COL_Z == 3 * DN_W and DN_W % PROJ_TN == 0
    return pl.pallas_call(
        _proj_kernel,
        grid=(T // tm, NP // PROJ_TN),
        in_specs=[pl.BlockSpec((tm, D_MODEL), lambda i, j: (i, 0)),
                  pl.BlockSpec((D_MODEL, PROJ_TN), lambda i, j: (0, j)),
                  pl.BlockSpec((DN_CONV, PROJ_TN), lambda i, j: (0, jnp.minimum(j, n_conv - 1)))],
        out_specs=[pl.BlockSpec((tm, PROJ_TN), lambda i, j: (i, j)),
                   pl.BlockSpec((tm, LANES), lambda i, j: (i, 0))],
        out_shape=[jax.ShapeDtypeStruct((T, NP), BF16), jax.ShapeDtypeStruct((T, LANES), F32)],
        scratch_shapes=[pltpu.VMEM((tm, D_MODEL), BF16)],
        compiler_params=_cparams(2),
        name="in_proj",
    )(x2d, wb, conv_w)


def _l2norm(x):
    return x * lax.rsqrt(jnp.sum(x * x, axis=-1, keepdims=True) + RMS_EPS)


def _deltanet_kernel(q_ref, k_ref, v_ref, z_ref, gates_ref, alog_ref, dtb_ref, ng_ref, o_ref,
                     gam_s, beta_s, n_s, t_s, p_s, qk_s, rhs_s, mq_s, kdt_s, qd_s, gt_s):
    S = q_ref.shape[0]
    nblk = S // DN_BLOCK
    hg = pl.program_id(1)
    blk = DN_BLOCK
    bc_s, o0_s = n_s, t_s

    @pl.when(hg == 0)
    def _():
        G = gates_ref[...]
        zz = G + dtb_ref[...]
        softplus = jnp.maximum(zz, 0.0) + jnp.log1p(jnp.exp(-jnp.abs(zz)))
        gam = -jnp.exp(alog_ref[...]) * softplus
        rowc = lax.broadcasted_iota(jnp.int32, G.shape, 0) & (blk - 1)
        sh = 1
        while sh < blk:
            gam = gam + jnp.where(rowc >= sh, pltpu.roll(gam, sh, 0), 0.0)
            sh *= 2
        gam_s[...] = gam
        beta_s[...] = _sigmoid(G)

    ri = lax.broadcasted_iota(jnp.int32, (blk, blk), 0)
    ci = lax.broadcasted_iota(jnp.int32, (blk, blk), 1)

    def same_block(b):
        s = int(math.log2(b))
        return (ri >> s) == (ci >> s)

    def setup(c, carry=None):
        rows = pl.ds(pl.multiple_of(c * blk, blk), blk)
        gsl = gam_s[rows, :]
        bsl = beta_s[rows, :]
        for i in range(DN_HPG):
            h = hg * DN_HPG + i
            cols = slice(i * HEAD_DIM, (i + 1) * HEAD_DIM)
            gam_col = jnp.sum(jnp.where(ci == h, gsl, 0.0), axis=1, keepdims=True)
            beta_col = jnp.sum(jnp.where(ci == h + DN_HEADS, bsl, 0.0), axis=1, keepdims=True)
            gb = jnp.broadcast_to(gam_col, (blk, HEAD_DIM))
            kn = _l2norm(k_ref[rows, cols].astype(F32))
            qn = _l2norm(q_ref[rows, cols].astype(F32)) * (HEAD_DIM ** -0.5)
            kb = kn * beta_col
            dec = jnp.exp(jnp.where(ri >= ci, gb - gb.T, -jnp.inf))
            a = _dot_nt(jnp.concatenate([qn, kb], axis=0).astype(BF16), kn.astype(BF16))
            qk_s[i, c] = (a[:blk] * dec).astype(BF16)
            n_s[i, c] = -a[blk:] * jnp.where(ri > ci, dec, 0.0)
            eg = jnp.exp(gb)
            rhs_s[i, c] = jnp.concatenate([v_ref[rows, cols].astype(F32) * beta_col, kb * eg], axis=1).astype(BF16)
            qd_s[i, c] = qn * eg
            gl = jnp.broadcast_to(gb[blk - 1:blk, :], (blk, HEAD_DIM))
            kdt_s[i, c] = (kn * jnp.exp(gl - gb)).T.astype(BF16)
            gt_s[i, c] = jnp.exp(gl[:8, :])
        return carry

    def series_start(i, c):
        d = jnp.where(same_block(INV_BASE), n_s[i, c], 0.0)
        return d, _dot(d.astype(BF16), d.astype(BF16))

    def series_start_done(i, c, r):
        t_s[i, c] = (ri == ci).astype(F32) + r[0]
        p_s[i, c] = r[1].astype(BF16)

    def series_both(i, c):
        pb = p_s[i, c]
        return _dot(jnp.concatenate([t_s[i, c].astype(BF16), pb], axis=0), pb)

    def series_both_done(i, c, r):
        t_s[i, c] = t_s[i, c] + r[:blk]
        p_s[i, c] = r[blk:].astype(BF16)

    def times_p(i, c):
        return _dot(t_s[i, c].astype(BF16), p_s[i, c])

    def add_to_t(i, c, r):
        t_s[i, c] = t_s[i, c] + r

    def to_p(i, c, r):
        p_s[i, c] = r.astype(BF16)

    stages = [(series_start, series_start_done)]
    stages += [(series_both, series_both_done)] * (int(math.log2(INV_BASE)) - 2)
    stages += [(times_p, add_to_t)]
    b = INV_BASE
    while b < blk:
        off = same_block(2 * b) & jnp.logical_not(same_block(b))

        def off_times_t(i, c, off=off):
            return _dot(jnp.where(off, n_s[i, c], 0.0).astype(BF16), t_s[i, c].astype(BF16))

        stages += [(off_times_t, to_p), (times_p, add_to_t)]
        b *= 2

    def solve(i, c):
        return _dot(t_s[i, c].astype(BF16), rhs_s[i, c])

    def solve_done(i, c, r):
        rhs_s[i, c] = r.astype(BF16)

    def cross(i, c):
        return _dot(jnp.concatenate([kdt_s[i, c], qk_s[i, c]], axis=0), rhs_s[i, c])

    def cross_done(i, c, r):
        bc_s[i, c] = r[:blk, :HEAD_DIM]
        o0_s[i, c] = r[blk:, :HEAD_DIM]
        mq_s[i, c] = jnp.concatenate([r[:blk, HEAD_DIM:], qd_s[i, c] - r[blk:, HEAD_DIM:]], axis=0).astype(BF16)

    stages += [(solve, solve_done), (cross, cross_done)]

    def run_stages(blocks, side_work):
        probs = [(i, c) for c in blocks for i in range(DN_HPG)]
        for k, (issue, finish) in enumerate(stages):
            results = [issue(i, c) for i, c in probs]
            if k < len(side_work):
                side_work[k]()
            for (i, c), r in zip(probs, results):
                finish(i, c, r)

    ng = ng_ref[...]

    def scan_step(c, states):
        rows = pl.ds(pl.multiple_of(c * blk, blk), blk)
        new_states = []
        for i in range(DN_HPG):
            cols = slice(i * HEAD_DIM, (i + 1) * HEAD_DIM)
            st = states[i]
            r = _dot(mq_s[i, c], st.astype(BF16))
            g_tot = jnp.broadcast_to(gt_s[i, c][0:1, :], (HEAD_DIM, HEAD_DIM))
            new_states.append(g_tot * st + bc_s[i, c] - r[:blk])
            o = o0_s[i, c] + r[blk:]
            o = o * lax.rsqrt(jnp.mean(o * o, axis=-1, keepdims=True) + RMS_EPS) * ng
            o_ref[rows, cols] = (o * z_ref[rows, cols].astype(F32)).astype(o_ref.dtype)
        return tuple(new_states)

    per = nblk // DN_PARTS
    parts = [list(range(p * per, (p + 1) * per)) for p in range(DN_PARTS)]
    lax.fori_loop(0, per, setup, 0)

    state = [tuple(jnp.zeros((HEAD_DIM, HEAD_DIM), F32) for _ in range(DN_HPG))]

    def scan_side(c):
        state[0] = scan_step(c, state[0])

    for p in range(DN_PARTS):
        side = []
        for j in range(per):
            if p + 1 < DN_PARTS:
                side.append(functools.partial(setup, parts[p + 1][j]))
            if p > 0:
                side.append(functools.partial(scan_side, parts[p - 1][j]))
        run_stages(parts[p], side)
    lax.fori_loop(nblk - per, nblk, scan_step, state[0])


def _deltanet(P3, G3, alog_row, dtb_row, ng_row):
    B, S, _ = P3.shape
    W = DN_HPG * HEAD_DIM
    ngrp = DN_HEADS // DN_HPG
    nblk = S // DN_BLOCK

    def col_spec(base):
        return pl.BlockSpec((None, S, W), lambda b, g: (b, 0, base // W + g))

    row_spec = pl.BlockSpec((1, LANES), lambda b, g: (0, 0))
    blk_buf = lambda rows, cols, dt: pltpu.VMEM((DN_HPG, nblk, rows, cols), dt)
    return pl.pallas_call(
        _deltanet_kernel,
        grid=(B, ngrp),
        in_specs=[col_spec(COL_QKV), col_spec(COL_QKV + DN_W), col_spec(COL_QKV + 2 * DN_W), col_spec(COL_Z),
                  pl.BlockSpec((None, S, LANES), lambda b, g: (b, 0, 0)),
                  row_spec, row_spec, row_spec],
        out_specs=pl.BlockSpec((None, S, W), lambda b, g: (b, 0, g)),
        out_shape=jax.ShapeDtypeStruct((B, S, DN_W), BF16),
        scratch_shapes=[pltpu.VMEM((S, LANES), F32), pltpu.VMEM((S, LANES), F32),
                        blk_buf(DN_BLOCK, DN_BLOCK, F32), blk_buf(DN_BLOCK, DN_BLOCK, F32),
                        blk_buf(DN_BLOCK, DN_BLOCK, BF16), blk_buf(DN_BLOCK, DN_BLOCK, BF16),
                        blk_buf(DN_BLOCK, 2 * HEAD_DIM, BF16),
                        blk_buf(2 * DN_BLOCK, HEAD_DIM, BF16),
                        blk_buf(HEAD_DIM, DN_BLOCK, BF16),
                        blk_buf(DN_BLOCK, HEAD_DIM, F32),
                        blk_buf(8, HEAD_DIM, F32)],
        compiler_params=_cparams(2),
        name="deltanet",
    )(P3, P3, P3, P3, G3, alog_row, dtb_row, ng_row)


def _dilated_kernel(q0_ref, q1_ref, q2_ref, k0_ref, k1_ref, k2_ref, v0_ref, v1_ref, v2_ref,
                    pos_ref, invf_ref, o_ref, cos_s, sin_s, qr_s, kr_s, vr_s, og_s, lg_s):
    S = q0_ref.shape[0]
    blk = DIL_BLOCK
    qscale = HEAD_DIM ** -0.5 * LOG2_E

    @pl.when(pl.program_id(1) == 0)
    def _():
        hs, hl = S // 2, HEAD_DIM // 2
        pos = pos_ref[...].astype(F32)
        lane = lax.broadcasted_iota(jnp.int32, (hs, HEAD_DIM), 1)
        low = lane < hl
        ang = jnp.where(low, pos[:hs, :], pos[hs:, :]) * invf_ref[...]
        c, sn = jnp.cos(ang), jnp.sin(ang)
        c_sw, sn_sw = pltpu.roll(c, hl, 1), pltpu.roll(sn, hl, 1)
        cos_s[pl.ds(0, hs), :] = jnp.where(low, c, c_sw)
        cos_s[pl.ds(hs, hs), :] = jnp.where(low, c_sw, c)
        sin_s[pl.ds(0, hs), :] = jnp.where(low, -sn, sn_sw)
        sin_s[pl.ds(hs, hs), :] = jnp.where(low, -sn_sw, sn)

    def rope(x):
        return x * cos_s[...] + pltpu.roll(x, HEAD_DIM // 2, 1) * sin_s[...]

    ri = lax.broadcasted_iota(jnp.int32, (blk, blk), 0)
    ci = lax.broadcasted_iota(jnp.int32, (blk, blk), 1)
    neg = -jnp.inf

    def rows_of(start, dil):
        return pl.ds(start, blk) if dil == 1 else pl.ds(start, blk, stride=dil)

    ones = jnp.ones((blk, HEAD_DIM), BF16)

    def block_group(g, starts, dil, has_prev):
        rows = [rows_of(st, dil) for st in starts]
        q = [qr_s[g, r, :].astype(BF16) for r in rows]
        s = [_dot_nt(q[j], kr_s[g, r, :].astype(BF16)) for j, r in enumerate(rows)]
        if has_prev:
            prows = [rows_of(st - blk * dil, dil) for st in starts]
            sp = [_dot_nt(q[j], kr_s[g, r, :].astype(BF16)) for j, r in enumerate(prows)]
        ps, pps, ms = [], [], []
        for j in range(len(starts)):
            sj = jnp.where(ri >= ci, s[j], neg)
            if has_prev:
                spj = jnp.where(ci >= ri, sp[j], neg)
                m = jnp.max(jnp.maximum(sj, spj), axis=-1, keepdims=True)
                pps.append(jnp.exp2(spj - m).astype(BF16))
            else:
                m = jnp.max(sj, axis=-1, keepdims=True)
            ps.append(jnp.exp2(sj - m).astype(BF16))
            ms.append(m)
        for j, r in enumerate(rows):
            ol = _dot(ps[j], jnp.concatenate([vr_s[g, r, :].astype(BF16), ones], axis=1))
            if has_prev:
                ol = ol + _dot(pps[j], jnp.concatenate([vr_s[g, prows[j], :].astype(BF16), ones], axis=1))
            l = ol[:, HEAD_DIM:]
            og_s[g, r, :] = ol[:, :HEAD_DIM] * (1.0 / l)
            lg_s[g, r, :] = ms[j] * (1.0 / LOG2_E) + jnp.log(l)

    q_refs = (q0_ref, q1_ref, q2_ref)
    k_refs = (k0_ref, k1_ref, k2_ref)
    v_refs = (v0_ref, v1_ref, v2_ref)
    for g in range(len(DIL_GROUPS)):
        qr_s[g] = rope(q_refs[g][...].astype(F32)) * qscale
        kr_s[g] = rope(k_refs[g][...].astype(F32))
        vr_s[g] = v_refs[g][...].astype(F32)
    for g, (window, dil) in enumerate(DIL_GROUPS):
        assert window // dil == blk
        nb = S // dil // blk
        block_group(g, list(range(dil)), dil, False)
        if nb > 1:
            block_group(g, [n * blk * dil + r for n in range(1, nb) for r in range(dil)], dil, True)

    l0, l1, l2 = lg_s[0], lg_s[1], lg_s[2]
    m = jnp.maximum(jnp.maximum(l0, l1), l2)
    e0, e1, e2 = jnp.exp(l0 - m), jnp.exp(l1 - m), jnp.exp(l2 - m)
    inv = 1.0 / (e0 + e1 + e2)
    o_ref[...] = ((e0 * inv) * og_s[0] + (e1 * inv) * og_s[1] + (e2 * inv) * og_s[2]).astype(o_ref.dtype)


def _dilated(P3, pos3, invf_row):
    B, S, _ = P3.shape

    def head_spec(base, g):
        return pl.BlockSpec((None, S, HEAD_DIM), lambda b, h: (b, 0, base // HEAD_DIM + g * DIL_HPG + h))

    ngr = len(DIL_GROUPS)
    in_specs = ([head_spec(COL_BQ, g) for g in range(ngr)] + [head_spec(COL_BK, g) for g in range(ngr)]
                + [head_spec(COL_BV, g) for g in range(ngr)]
                + [pl.BlockSpec((None, S, 1), lambda b, h: (b, 0, 0)),
                   pl.BlockSpec((1, HEAD_DIM), lambda b, h: (0, 0))])
    big = pltpu.VMEM((S, HEAD_DIM), F32)
    return pl.pallas_call(
        _dilated_kernel,
        grid=(B, DIL_HPG),
        in_specs=in_specs,
        out_specs=pl.BlockSpec((None, S, HEAD_DIM), lambda b, h: (b, 0, h)),
        out_shape=jax.ShapeDtypeStruct((B, S, DIL_OUT), BF16),
        scratch_shapes=[big, big] + [pltpu.VMEM((ngr, S, HEAD_DIM), F32)] * 5,
        compiler_params=_cparams(2),
        name="dilated",
    )(*([P3] * 9), pos3, invf_row)


def _mix_kernel(oa_ref, ob_ref, ga_ref, gb_ref, x_ref, wa_ref, wb_ref, wm_ref, g_ref, b_ref, o_ref):
    subs = [slice(r, r + ROW_SUB) for r in range(0, x_ref.shape[0], ROW_SUB)]
    ya = [_dot(oa_ref[r, :], wa_ref[...]) for r in subs]
    yb = [_dot(ob_ref[r, :], wb_ref[...]) for r in subs]
    hs = [(_sigmoid(ga_ref[r, :].astype(F32)) * a + _sigmoid(gb_ref[r, :].astype(F32)) * b).astype(BF16)
          for r, a, b in zip(subs, ya, yb)]
    mix = [_dot(h, wm_ref[...]) for h in hs]
    for r, m in zip(subs, mix):
        o_ref[r, :] = _layer_norm(ALPHA * x_ref[r, :] + m, g_ref[...], b_ref[...])


def _const_spec(shape):
    return pl.BlockSpec(shape, lambda *_: (0,) * len(shape), pipeline_mode=pl.Buffered(1))


def _mix(oa, ob, P, x2d, wa, wb, wm, g, b, tm):
    T = x2d.shape[0]
    row = lambda w, j=0: pl.BlockSpec((tm, w), lambda i: (i, j))
    return pl.pallas_call(
        _mix_kernel,
        grid=(T // tm,),
        in_specs=[row(DN_W), row(DIL_OUT), row(D_MODEL, COL_GA // D_MODEL), row(D_MODEL, COL_GB // D_MODEL),
                  row(D_MODEL), _const_spec(wa.shape), _const_spec(wb.shape), _const_spec(wm.shape),
                  _const_spec(g.shape), _const_spec(b.shape)],
        out_specs=row(D_MODEL),
        out_shape=jax.ShapeDtypeStruct((T, D_MODEL), F32),
        compiler_params=_cparams(1),
        name="mix",
    )(oa, ob, P, P, x2d, wa, wb, wm, g, b)


def _memkv_kernel(m_ref, w_ref, o_ref):
    o_ref[...] = _dot(m_ref[...].astype(BF16), w_ref[...]).astype(o_ref.dtype)


def _mem_kv(mem2d, wkv, tm):
    R = mem2d.shape[0]
    return pl.pallas_call(
        _memkv_kernel,
        grid=(R // tm,),
        in_specs=[pl.BlockSpec((tm, D_MODEL), lambda i: (i, 0)), _const_spec(wkv.shape)],
        out_specs=pl.BlockSpec((tm, 2 * D_MODEL), lambda i: (i, 0)),
        out_shape=jax.ShapeDtypeStruct((R, 2 * D_MODEL), BF16),
        compiler_params=_cparams(1),
        name="mem_kv",
    )(mem2d, wkv)


def _xattn_kernel(x_ref, kv_ref, wq_ref, wo_ref, g_ref, b_ref, o_ref):
    tm = x_ref.shape[0]
    subs = [slice(r, r + ROW_SUB) for r in range(0, tm, ROW_SUB)]
    scale = MEM_HEAD_DIM ** -0.5
    heads = [(slice(h * MEM_HEAD_DIM, (h + 1) * MEM_HEAD_DIM),
              slice(D_MODEL + h * MEM_HEAD_DIM, D_MODEL + (h + 1) * MEM_HEAD_DIM)) for h in range(MEM_HEADS)]
    xs = [x_ref[r, :] for r in subs]
    qs = [_dot(x.astype(BF16), wq_ref[...]) for x in xs]
    ss = [[_dot_nt(q[:, kc].astype(BF16), kv_ref[:, kc]) * scale for kc, _ in heads] for q in qs]
    ps = []
    for s_sub in ss:
        row = []
        for s in s_sub:
            p = jnp.exp(s - jnp.max(s, axis=-1, keepdims=True))
            row.append((p * (1.0 / jnp.sum(p, axis=-1, keepdims=True))).astype(BF16))
        ps.append(row)
    os = [jnp.concatenate([_dot(p, kv_ref[:, vc]).astype(BF16) for p, (_, vc) in zip(p_sub, heads)], axis=1)
          for p_sub in ps]
    cross = [_dot(o, wo_ref[...]) for o in os]
    for r, x, c in zip(subs, xs, cross):
        o_ref[r, :] = _layer_norm(ALPHA * x + c, g_ref[...], b_ref[...])


def _xattn(x3, kv3, wq, wo, g, b, tm):
    B, S, _ = x3.shape
    M = kv3.shape[1]
    return pl.pallas_call(
        _xattn_kernel,
        grid=(B, S // tm),
        in_specs=[pl.BlockSpec((None, tm, D_MODEL), lambda bi, i: (bi, i, 0)),
                  pl.BlockSpec((None, M, 2 * D_MODEL), lambda bi, i: (bi, 0, 0)),
                  _const_spec(wq.shape), _const_spec(wo.shape), _const_spec(g.shape), _const_spec(b.shape)],
        out_specs=pl.BlockSpec((None, tm, D_MODEL), lambda bi, i: (bi, i, 0)),
        out_shape=jax.ShapeDtypeStruct((B, S, D_MODEL), F32),
        compiler_params=_cparams(2),
        name="xattn",
    )(x3, kv3, wq, wo, g, b)


def _ffn_kernel(x_ref, w13_ref, w2_ref, g_ref, b_ref, o_ref):
    x = x_ref[...]
    xb = x.astype(BF16)
    acc = jnp.zeros(x.shape, F32)
    for j in range(D_FF // FF_CHUNK):
        ca = slice(j * FF_CHUNK, (j + 1) * FF_CHUNK)
        cb = slice(D_FF + j * FF_CHUNK, D_FF + (j + 1) * FF_CHUNK)
        a = _dot(xb, w13_ref[:, ca])
        h = (a * _sigmoid(a)) * _dot(xb, w13_ref[:, cb])
        acc = acc + _dot(h.astype(BF16), w2_ref[ca, :])
    o_ref[...] = _layer_norm(ALPHA * x + acc, g_ref[...], b_ref[...])


def _ffn(x2d, w13, w2, g, b, tm):
    T = x2d.shape[0]
    return pl.pallas_call(
        _ffn_kernel,
        grid=(T // tm,),
        in_specs=[pl.BlockSpec((tm, D_MODEL), lambda i: (i, 0)),
                  _const_spec(w13.shape), _const_spec(w2.shape), _const_spec(g.shape), _const_spec(b.shape)],
        out_specs=pl.BlockSpec((tm, D_MODEL), lambda i: (i, 0)),
        out_shape=jax.ShapeDtypeStruct((T, D_MODEL), F32),
        compiler_params=_cparams(1),
        name="ffn",
    )(x2d, w13, w2, g, b)


def _relayout_w_in(w):
    sizes = (3 * DN_W, DN_W, DN_HEADS, DN_HEADS, DIL_W, DIL_W, DIL_W, D_MODEL, D_MODEL)
    offs = [0]
    for s in sizes:
        offs.append(offs[-1] + s)
    part = lambda i: w[:, offs[i]:offs[i + 1]]
    used = COL_GATES + 2 * DN_HEADS
    pad = jnp.zeros((w.shape[0], NP - used), w.dtype)
    tail = w[:, offs[4]:]
    tpart = lambda i: tail[:, offs[i] - offs[4]:offs[i + 1] - offs[4]]
    return jnp.concatenate([part(0), part(1), tpart(7), tpart(8), tpart(4), tpart(5), tpart(6),
                            part(2), part(3), pad], axis=1)


def _pad_row(v, fill=0.0):
    return jnp.concatenate([v.astype(F32), jnp.full((LANES - v.shape[0],), fill, F32)])[None, :]


def kernel(x, mem, positions, w_in, conv_w, a_log, dt_bias, dn_norm_g, w_br_a, w_br_b, w_mix_out,
           ln1_g, ln1_b, w_xq, w_xkv, w_xo, ln2_g, ln2_b, w_ffn13, w_ffn2, ln3_g, ln3_b):
    B, S, D = x.shape
    T = B * S
    M = mem.shape[1]
    assert D == D_MODEL and w_in.shape[0] == 1 and S % (DIL_BLOCK * DIL_GROUPS[-1][1]) == 0
    x2d = x.reshape(T, D)
    bf = lambda w: w[0].astype(BF16)
    row = lambda v: v[0][None, :].astype(F32)

    P, G = _in_proj(x2d, _relayout_w_in(w_in[0]).astype(BF16), conv_w[0], tm=S)
    P3 = P.reshape(B, S, NP)

    o_a = _deltanet(P3, G.reshape(B, S, LANES), _pad_row(a_log[0]), _pad_row(dt_bias[0]), row(dn_norm_g))

    inv_freq = ROPE_THETA ** (-jnp.arange(0, HEAD_DIM, 2, dtype=F32) / HEAD_DIM)
    invf_row = jnp.concatenate([inv_freq, inv_freq])[None, :]
    o_b = _dilated(P3, positions.reshape(B, S, 1), invf_row)

    x1 = _mix(o_a.reshape(T, DN_W), o_b.reshape(T, DIL_OUT), P, x2d, bf(w_br_a), bf(w_br_b), bf(w_mix_out),
              row(ln1_g), row(ln1_b), tm=2 * ROW_SUB)

    kv = _mem_kv(mem.reshape(B * M, D), bf(w_xkv), tm=512)
    x2 = _xattn(x1.reshape(B, S, D), kv.reshape(B, M, 2 * D), bf(w_xq), bf(w_xo), row(ln2_g), row(ln2_b), tm=2 * ROW_SUB)

    out = _ffn(x2.reshape(T, D), bf(w_ffn13), bf(w_ffn2), row(ln3_g), row(ln3_b), tm=512)
    return out.reshape(B, S, D)
```

```python
import functools
import math

import jax
import jax.numpy as jnp
from jax import lax
from jax.experimental import pallas as pl
from jax.experimental.pallas import tpu as pltpu

F32 = jnp.float32
BF16 = jnp.bfloat16

LANES = 128
D_MODEL = 1024
DN_HEADS = 8
HEAD_DIM = 128
DN_W = DN_HEADS * HEAD_DIM
DN_CONV = 4
DN_BLOCK = 128
DN_HPG = 2
INV_BASE = 16
DN_PART_BLOCKS = (2, 5, 5, 4)
DIL_GROUPS = ((128, 1), (512, 4), (2048, 16))
DIL_HPG = 4
DIL_W = len(DIL_GROUPS) * DIL_HPG * HEAD_DIM
DIL_OUT = DIL_HPG * HEAD_DIM
DIL_BLOCK = 128
ROPE_THETA = 10000.0
MEM_HEADS = 4
MEM_HEAD_DIM = D_MODEL // MEM_HEADS
D_FF = 2816
FF_CHUNK = 256
ROW_SUB = 512
ALPHA = 2.0 ** 0.25
LN_EPS = 1e-5
RMS_EPS = 1e-6
LOG2_E = 1.4426950408889634

COL_QKV = 0
COL_Z = 3 * DN_W
COL_GA = COL_Z + DN_W
COL_GB = COL_GA + D_MODEL
COL_BQ = COL_GB + D_MODEL
COL_BK = COL_BQ + DIL_W
COL_BV = COL_BK + DIL_W
COL_GATES = COL_BV + DIL_W
PROJ_TN = 1024
PROJ_CONV_ROWS = 128
NP = 11264

VMEM_LIMIT = 52 * 1024 * 1024


def _cparams(n_axes):
    return pltpu.CompilerParams(dimension_semantics=("arbitrary",) * n_axes,
                                vmem_limit_bytes=VMEM_LIMIT)


def _dot(a, b):
    return jnp.dot(a, b, preferred_element_type=F32)


def _dot_nt(a, b):
    return lax.dot_general(a, b, (((1,), (1,)), ((), ())), preferred_element_type=F32)


def _sigmoid(x):
    return 1.0 / (1.0 + jnp.exp2(x * -LOG2_E))


def _silu(x):
    return x * _sigmoid(x)


def _layer_norm(y, g, b):
    mu = jnp.mean(y, axis=-1, keepdims=True)
    yc = y - mu
    var = jnp.mean(yc * yc, axis=-1, keepdims=True)
    return yc * lax.rsqrt(var + LN_EPS) * g + b


def _proj_kernel(x_ref, w_ref, cw_ref, o_ref, gates_ref, xb_s):
    S = x_ref.shape[0]
    j = pl.program_id(1)
    n_conv = 3 * DN_W // PROJ_TN

    @pl.when(j == 0)
    def _():
        xb_s[...] = x_ref[...].astype(BF16)

    @pl.when(j < n_conv)
    def _():
        halo = 8
        w = w_ref[...]
        cw = cw_ref[...]
        tail = jnp.zeros((halo, PROJ_TN), F32)
        n_chunks = S // PROJ_CONV_ROWS
        chunk = lambda c: _dot(xb_s[c * PROJ_CONV_ROWS:(c + 1) * PROJ_CONV_ROWS, :], w)
        nxt = chunk(0)
        for c in range(n_chunks):
            rows = slice(c * PROJ_CONV_ROWS, (c + 1) * PROJ_CONV_ROWS)
            acc = nxt
            if c + 1 < n_chunks:
                nxt = chunk(c + 1)
            win = jnp.concatenate([tail, acc], axis=0)
            tail = acc[PROJ_CONV_ROWS - halo:, :]
            y = acc * cw[DN_CONV - 1:DN_CONV, :]
            for t in range(DN_CONV - 1):
                y = y + pltpu.roll(win, DN_CONV - 1 - t, 0)[halo:, :] * cw[t:t + 1, :]
            o_ref[rows, :] = _silu(y).astype(o_ref.dtype)

    @pl.when(j == COL_Z // PROJ_TN)
    def _():
        w = w_ref[...]
        n_chunks = S // PROJ_CONV_ROWS
        chunk = lambda c: _dot(xb_s[c * PROJ_CONV_ROWS:(c + 1) * PROJ_CONV_ROWS, :], w)
        nxt = chunk(0)
        for c in range(n_chunks):
            acc = nxt
            if c + 1 < n_chunks:
                nxt = chunk(c + 1)
            o_ref[c * PROJ_CONV_ROWS:(c + 1) * PROJ_CONV_ROWS, :] = _silu(acc).astype(o_ref.dtype)

    @pl.when(j > COL_Z // PROJ_TN)
    def _():
        acc = _dot(xb_s[...], w_ref[...])
        o_ref[...] = acc.astype(o_ref.dtype)

        @pl.when(j == COL_GATES // PROJ_TN)
        def _():
            lo = COL_GATES % PROJ_TN
            gates_ref[...] = acc[:, lo:lo + LANES]


def _in_proj(x2d, wb, conv_w, tm):
    T = x2d.shape[0]
    n_conv = 3 * DN_W // PROJ_TN
    assert COL_QKV == 0 and COL_Z == 3 * DN_W and DN_W % PROJ_TN == 0
    return pl.pallas_call(
        _proj_kernel,
        grid=(T // tm, NP // PROJ_TN),
        in_specs=[pl.BlockSpec((tm, D_MODEL), lambda i, j: (i, 0)),
                  pl.BlockSpec((D_MODEL, PROJ_TN), lambda i, j: (0, j)),
                  pl.BlockSpec((DN_CONV, PROJ_TN), lambda i, j: (0, jnp.minimum(j, n_conv - 1)))],
        out_specs=[pl.BlockSpec((tm, PROJ_TN), lambda i, j: (i, j)),
                   pl.BlockSpec((tm, LANES), lambda i, j: (i, 0))],
        out_shape=[jax.ShapeDtypeStruct((T, NP), BF16), jax.ShapeDtypeStruct((T, LANES), F32)],
        scratch_shapes=[pltpu.VMEM((tm, D_MODEL), BF16)],
        compiler_params=_cparams(2),
        name="in_proj",
    )(x2d, wb, conv_w)


def _l2norm(x):
    return x * lax.rsqrt(jnp.sum(x * x, axis=-1, keepdims=True) + RMS_EPS)


def _deltanet_kernel(q_ref, k_ref, v_ref, z_ref, gates_ref, alog_ref, dtb_ref, ng_ref, o_ref,
                     gam_s, beta_s, n_s, t_s, p_s, qk_s, rhs_s, mq_s, kdt_s, qd_s, gt_s):
    S = q_ref.shape[0]
    nblk = S // DN_BLOCK
    hg = pl.program_id(1)
    blk = DN_BLOCK
    bc_s, o0_s = n_s, t_s

    @pl.when(hg == 0)
    def _():
        G = gates_ref[...]
        zz = G + dtb_ref[...]
        softplus = jnp.maximum(zz, 0.0) + jnp.log1p(jnp.exp(-jnp.abs(zz)))
        gam = -jnp.exp(alog_ref[...]) * softplus
        rowc = lax.broadcasted_iota(jnp.int32, G.shape, 0) & (blk - 1)
        sh = 1
        while sh < blk:
            gam = gam + jnp.where(rowc >= sh, pltpu.roll(gam, sh, 0), 0.0)
            sh *= 2
        gam_s[...] = gam
        beta_s[...] = _sigmoid(G)

    ri = lax.broadcasted_iota(jnp.int32, (blk, blk), 0)
    ci = lax.broadcasted_iota(jnp.int32, (blk, blk), 1)

    def same_block(b):
        s = int(math.log2(b))
        return (ri >> s) == (ci >> s)

    def setup(c, carry=None):
        rows = pl.ds(pl.multiple_of(c * blk, blk), blk)
        gsl = gam_s[rows, :]
        bsl = beta_s[rows, :]
        for i in range(DN_HPG):
            h = hg * DN_HPG + i
            cols = slice(i * HEAD_DIM, (i + 1) * HEAD_DIM)
            gam_col = jnp.sum(jnp.where(ci == h, gsl, 0.0), axis=1, keepdims=True)
            beta_col = jnp.sum(jnp.where(ci == h + DN_HEADS, bsl, 0.0), axis=1, keepdims=True)
            gb = jnp.broadcast_to(gam_col, (blk, HEAD_DIM))
            kn = _l2norm(k_ref[rows, cols].astype(F32))
            qn = _l2norm(q_ref[rows, cols].astype(F32)) * (HEAD_DIM ** -0.5)
            kb = kn * beta_col
            dec = jnp.exp(jnp.where(ri >= ci, gb - gb.T, -jnp.inf))
            a = _dot_nt(jnp.concatenate([qn, kb], axis=0).astype(BF16), kn.astype(BF16))
            qk_s[i, c] = (a[:blk] * dec).astype(BF16)
            n_s[i, c] = -a[blk:] * jnp.where(ri > ci, dec, 0.0)
            eg = jnp.exp(gb)
            rhs_s[i, c] = jnp.concatenate([v_ref[rows, cols].astype(F32) * beta_col, kb * eg], axis=1).astype(BF16)
            qd_s[i, c] = qn * eg
            gl = jnp.broadcast_to(gb[blk - 1:blk, :], (blk, HEAD_DIM))
            kdt_s[i, c] = (kn * jnp.exp(gl - gb)).T.astype(BF16)
            gt_s[i, c] = jnp.exp(gl[:8, :])
        return carry

    def series_start(i, c):
        d = jnp.where(same_block(INV_BASE), n_s[i, c], 0.0)
        return d, _dot(d.astype(BF16), d.astype(BF16))

    def series_start_done(i, c, r):
        t_s[i, c] = (ri == ci).astype(F32) + r[0]
        p_s[i, c] = r[1].astype(BF16)

    def series_both(i, c):
        pb = p_s[i, c]
        return _dot(jnp.concatenate([t_s[i, c].astype(BF16), pb], axis=0), pb)

    def series_both_done(i, c, r):
        t_s[i, c] = t_s[i, c] + r[:blk]
        p_s[i, c] = r[blk:].astype(BF16)

    def times_p(i, c):
        return _dot(t_s[i, c].astype(BF16), p_s[i, c])

    def add_to_t(i, c, r):
        t_s[i, c] = t_s[i, c] + r

    def to_p(i, c, r):
        p_s[i, c] = r.astype(BF16)

    stages = [(series_start, series_start_done)]
    stages += [(series_both, series_both_done)] * (int(math.log2(INV_BASE)) - 2)
    stages += [(times_p, add_to_t)]
    b = INV_BASE
    while b < blk:
        off = same_block(2 * b) & jnp.logical_not(same_block(b))

        def off_times_t(i, c, off=off):
            return _dot(jnp.where(off, n_s[i, c], 0.0).astype(BF16), t_s[i, c].astype(BF16))

        stages += [(off_times_t, to_p), (times_p, add_to_t)]
        b *= 2

    def solve(i, c):
        return _dot(t_s[i, c].astype(BF16), rhs_s[i, c])

    def solve_done(i, c, r):
        rhs_s[i, c] = r.astype(BF16)

    def cross(i, c):
        return _dot(jnp.concatenate([kdt_s[i, c], qk_s[i, c]], axis=0), rhs_s[i, c])

    def cross_done(i, c, r):
        bc_s[i, c] = r[:blk, :HEAD_DIM]
        o0_s[i, c] = r[blk:, :HEAD_DIM]
        mq_s[i, c] = jnp.concatenate([r[:blk, HEAD_DIM:], qd_s[i, c] - r[blk:, HEAD_DIM:]], axis=0).astype(BF16)

    stages += [(solve, solve_done), (cross, cross_done)]

    def run_stages(blocks, side_work):
        probs = [(i, c) for c in blocks for i in range(DN_HPG)]
        for k, (issue, finish) in enumerate(stages):
            results = [issue(i, c) for i, c in probs]
            if k < len(side_work):
                side_work[k]()
            for (i, c), r in zip(probs, results):
                finish(i, c, r)

    ng = ng_ref[...]

    def scan_step(c, states):
        rows = pl.ds(pl.multiple_of(c * blk, blk), blk)
        new_states = []
        for i in range(DN_HPG):
            cols = slice(i * HEAD_DIM, (i + 1) * HEAD_DIM)
            st = states[i]
            r = _dot(mq_s[i, c], st.astype(BF16))
            g_tot = jnp.broadcast_to(gt_s[i, c][0:1, :], (HEAD_DIM, HEAD_DIM))
            new_states.append(g_tot * st + bc_s[i, c] - r[:blk])
            o = o0_s[i, c] + r[blk:]
            o = o * lax.rsqrt(jnp.mean(o * o, axis=-1, keepdims=True) + RMS_EPS) * ng
            o_ref[rows, cols] = (o * z_ref[rows, cols].astype(F32)).astype(o_ref.dtype)
        return tuple(new_states)

    assert sum(DN_PART_BLOCKS) == nblk
    starts = [sum(DN_PART_BLOCKS[:p]) for p in range(len(DN_PART_BLOCKS) + 1)]
    parts = [list(range(starts[p], starts[p + 1])) for p in range(len(DN_PART_BLOCKS))]
    lax.fori_loop(0, len(parts[0]), setup, 0)

    state = [tuple(jnp.zeros((HEAD_DIM, HEAD_DIM), F32) for _ in range(DN_HPG))]

    def scan_side(c):
        state[0] = scan_step(c, state[0])

    for p in range(len(parts)):
        side = [functools.partial(setup, c) for c in (parts[p + 1] if p + 1 < len(parts) else [])]
        side += [functools.partial(scan_side, c) for c in (parts[p - 1] if p > 0 else [])]
        assert len(side) <= len(stages)
        run_stages(parts[p], side)
    lax.fori_loop(parts[-1][0], nblk, scan_step, state[0])


def _deltanet(P3, G3, alog_row, dtb_row, ng_row):
    B, S, _ = P3.shape
    W = DN_HPG * HEAD_DIM
    ngrp = DN_HEADS // DN_HPG
    nblk = S // DN_BLOCK

    def col_spec(base):
        return pl.BlockSpec((None, S, W), lambda b, g: (b, 0, base // W + g))

    row_spec = pl.BlockSpec((1, LANES), lambda b, g: (0, 0))
    blk_buf = lambda rows, cols, dt: pltpu.VMEM((DN_HPG, nblk, rows, cols), dt)
    return pl.pallas_call(
        _deltanet_kernel,
        grid=(B, ngrp),
        in_specs=[col_spec(COL_QKV), col_spec(COL_QKV + DN_W), col_spec(COL_QKV + 2 * DN_W), col_spec(COL_Z),
                  pl.BlockSpec((None, S, LANES), lambda b, g: (b, 0, 0)),
                  row_spec, row_spec, row_spec],
        out_specs=pl.BlockSpec((None, S, W), lambda b, g: (b, 0, g)),
        out_shape=jax.ShapeDtypeStruct((B, S, DN_W), BF16),
        scratch_shapes=[pltpu.VMEM((S, LANES), F32), pltpu.VMEM((S, LANES), F32),
                        blk_buf(DN_BLOCK, DN_BLOCK, F32), blk_buf(DN_BLOCK, DN_BLOCK, F32),
                        blk_buf(DN_BLOCK, DN_BLOCK, BF16), blk_buf(DN_BLOCK, DN_BLOCK, BF16),
                        blk_buf(DN_BLOCK, 2 * HEAD_DIM, BF16),
                        blk_buf(2 * DN_BLOCK, HEAD_DIM, BF16),
                        blk_buf(HEAD_DIM, DN_BLOCK, BF16),
                        blk_buf(DN_BLOCK, HEAD_DIM, F32),
                        blk_buf(8, HEAD_DIM, F32)],
        compiler_params=_cparams(2),
        name="deltanet",
    )(P3, P3, P3, P3, G3, alog_row, dtb_row, ng_row)


def _dilated_kernel(q0_ref, q1_ref, q2_ref, k0_ref, k1_ref, k2_ref, v0_ref, v1_ref, v2_ref,
                    pos_ref, invf_ref, o_ref, cos_s, sin_s, qr_s, kr_s, vr_s, og_s, lg_s):
    S = q0_ref.shape[0]
    blk = DIL_BLOCK
    qscale = HEAD_DIM ** -0.5 * LOG2_E

    @pl.when(pl.program_id(1) == 0)
    def _():
        hs, hl = S // 2, HEAD_DIM // 2
        pos = pos_ref[...].astype(F32)
        lane = lax.broadcasted_iota(jnp.int32, (hs, HEAD_DIM), 1)
        low = lane < hl
        ang = jnp.where(low, pos[:hs, :], pos[hs:, :]) * invf_ref[...]
        c, sn = jnp.cos(ang), jnp.sin(ang)
        c_sw, sn_sw = pltpu.roll(c, hl, 1), pltpu.roll(sn, hl, 1)
        cos_s[pl.ds(0, hs), :] = jnp.where(low, c, c_sw)
        cos_s[pl.ds(hs, hs), :] = jnp.where(low, c_sw, c)
        sin_s[pl.ds(0, hs), :] = jnp.where(low, -sn, sn_sw)
        sin_s[pl.ds(hs, hs), :] = jnp.where(low, -sn_sw, sn)

    def rope(x):
        return x * cos_s[...] + pltpu.roll(x, HEAD_DIM // 2, 1) * sin_s[...]

    ri = lax.broadcasted_iota(jnp.int32, (blk, blk), 0)
    ci = lax.broadcasted_iota(jnp.int32, (blk, blk), 1)
    neg = -jnp.inf

    def rows_of(start, dil):
        return pl.ds(start, blk) if dil == 1 else pl.ds(start, blk, stride=dil)

    ones = jnp.ones((blk, HEAD_DIM), BF16)

    def block_group(g, starts, dil, has_prev):
        rows = [rows_of(st, dil) for st in starts]
        q = [qr_s[g, r, :].astype(BF16) for r in rows]
        s = [_dot_nt(q[j], kr_s[g, r, :].astype(BF16)) for j, r in enumerate(rows)]
        if has_prev:
            prows = [rows_of(st - blk * dil, dil) for st in starts]
            sp = [_dot_nt(q[j], kr_s[g, r, :].astype(BF16)) for j, r in enumerate(prows)]
        ps, pps, ms = [], [], []
        for j in range(len(starts)):
            sj = jnp.where(ri >= ci, s[j], neg)
            if has_prev:
                spj = jnp.where(ci >= ri, sp[j], neg)
                m = jnp.max(jnp.maximum(sj, spj), axis=-1, keepdims=True)
                pps.append(jnp.exp2(spj - m).astype(BF16))
            else:
                m = jnp.max(sj, axis=-1, keepdims=True)
            ps.append(jnp.exp2(sj - m).astype(BF16))
            ms.append(m)
        for j, r in enumerate(rows):
            ol = _dot(ps[j], jnp.concatenate([vr_s[g, r, :].astype(BF16), ones], axis=1))
            if has_prev:
                ol = ol + _dot(pps[j], jnp.concatenate([vr_s[g, prows[j], :].astype(BF16), ones], axis=1))
            l = ol[:, HEAD_DIM:]
            og_s[g, r, :] = ol[:, :HEAD_DIM] * (1.0 / l)
            lg_s[g, r, :] = ms[j] * (1.0 / LOG2_E) + jnp.log(l)

    q_refs = (q0_ref, q1_ref, q2_ref)
    k_refs = (k0_ref, k1_ref, k2_ref)
    v_refs = (v0_ref, v1_ref, v2_ref)
    for g in range(len(DIL_GROUPS)):
        qr_s[g] = rope(q_refs[g][...].astype(F32)) * qscale
        kr_s[g] = rope(k_refs[g][...].astype(F32))
        vr_s[g] = v_refs[g][...].astype(F32)
    for g, (window, dil) in enumerate(DIL_GROUPS):
        assert window // dil == blk
        nb = S // dil // blk
        block_group(g, list(range(dil)), dil, False)
        if nb > 1:
            block_group(g, [n * blk * dil + r for n in range(1, nb) for r in range(dil)], dil, True)

    l0, l1, l2 = lg_s[0], lg_s[1], lg_s[2]
    m = jnp.maximum(jnp.maximum(l0, l1), l2)
    e0, e1, e2 = jnp.exp(l0 - m), jnp.exp(l1 - m), jnp.exp(l2 - m)
    inv = 1.0 / (e0 + e1 + e2)
    o_ref[...] = ((e0 * inv) * og_s[0] + (e1 * inv) * og_s[1] + (e2 * inv) * og_s[2]).astype(o_ref.dtype)


def _dilated(P3, pos3, invf_row):
    B, S, _ = P3.shape

    def head_spec(base, g):
        return pl.BlockSpec((None, S, HEAD_DIM), lambda b, h: (b, 0, base // HEAD_DIM + g * DIL_HPG + h))

    ngr = len(DIL_GROUPS)
    in_specs = ([head_spec(COL_BQ, g) for g in range(ngr)] + [head_spec(COL_BK, g) for g in range(ngr)]
                + [head_spec(COL_BV, g) for g in range(ngr)]
                + [pl.BlockSpec((None, S, 1), lambda b, h: (b, 0, 0)),
                   pl.BlockSpec((1, HEAD_DIM), lambda b, h: (0, 0))])
    big = pltpu.VMEM((S, HEAD_DIM), F32)
    return pl.pallas_call(
        _dilated_kernel,
        grid=(B, DIL_HPG),
        in_specs=in_specs,
        out_specs=pl.BlockSpec((None, S, HEAD_DIM), lambda b, h: (b, 0, h)),
        out_shape=jax.ShapeDtypeStruct((B, S, DIL_OUT), BF16),
        scratch_shapes=[big, big] + [pltpu.VMEM((ngr, S, HEAD_DIM), F32)] * 5,
        compiler_params=_cparams(2),
        name="dilated",
    )(*([P3] * 9), pos3, invf_row)


def _mix_kernel(oa_ref, ob_ref, ga_ref, gb_ref, x_ref, wa_ref, wb_ref, wm_ref, g_ref, b_ref, o_ref):
    subs = [slice(r, r + ROW_SUB) for r in range(0, x_ref.shape[0], ROW_SUB)]
    ya = [_dot(oa_ref[r, :], wa_ref[...]) for r in subs]
    yb = [_dot(ob_ref[r, :], wb_ref[...]) for r in subs]
    hs = [(_sigmoid(ga_ref[r, :].astype(F32)) * a + _sigmoid(gb_ref[r, :].astype(F32)) * b).astype(BF16)
          for r, a, b in zip(subs, ya, yb)]
    mix = [_dot(h, wm_ref[...]) for h in hs]
    for r, m in zip(subs, mix):
        o_ref[r, :] = _layer_norm(ALPHA * x_ref[r, :] + m, g_ref[...], b_ref[...])


def _const_spec(shape):
    return pl.BlockSpec(shape, lambda *_: (0,) * len(shape), pipeline_mode=pl.Buffered(1))


def _mix(oa, ob, P, x2d, wa, wb, wm, g, b, tm):
    T = x2d.shape[0]
    row = lambda w, j=0: pl.BlockSpec((tm, w), lambda i: (i, j))
    return pl.pallas_call(
        _mix_kernel,
        grid=(T // tm,),
        in_specs=[row(DN_W), row(DIL_OUT), row(D_MODEL, COL_GA // D_MODEL), row(D_MODEL, COL_GB // D_MODEL),
                  row(D_MODEL), _const_spec(wa.shape), _const_spec(wb.shape), _const_spec(wm.shape),
                  _const_spec(g.shape), _const_spec(b.shape)],
        out_specs=row(D_MODEL),
        out_shape=jax.ShapeDtypeStruct((T, D_MODEL), F32),
        compiler_params=_cparams(1),
        name="mix",
    )(oa, ob, P, P, x2d, wa, wb, wm, g, b)


def _memkv_kernel(m_ref, w_ref, o_ref):
    o_ref[...] = _dot(m_ref[...].astype(BF16), w_ref[...]).astype(o_ref.dtype)


def _mem_kv(mem2d, wkv, tm):
    R = mem2d.shape[0]
    return pl.pallas_call(
        _memkv_kernel,
        grid=(R // tm,),
        in_specs=[pl.BlockSpec((tm, D_MODEL), lambda i: (i, 0)), _const_spec(wkv.shape)],
        out_specs=pl.BlockSpec((tm, 2 * D_MODEL), lambda i: (i, 0)),
        out_shape=jax.ShapeDtypeStruct((R, 2 * D_MODEL), BF16),
        compiler_params=_cparams(1),
        name="mem_kv",
    )(mem2d, wkv)


def _xattn_kernel(x_ref, kv_ref, wq_ref, wo_ref, g_ref, b_ref, o_ref):
    tm = x_ref.shape[0]
    subs = [slice(r, r + ROW_SUB) for r in range(0, tm, ROW_SUB)]
    scale = MEM_HEAD_DIM ** -0.5
    heads = [(slice(h * MEM_HEAD_DIM, (h + 1) * MEM_HEAD_DIM),
              slice(D_MODEL + h * MEM_HEAD_DIM, D_MODEL + (h + 1) * MEM_HEAD_DIM)) for h in range(MEM_HEADS)]
    xs = [x_ref[r, :] for r in subs]
    qs = [_dot(x.astype(BF16), wq_ref[...]) for x in xs]
    ss = [[_dot_nt(q[:, kc].astype(BF16), kv_ref[:, kc]) * scale for kc, _ in heads] for q in qs]
    ps = []
    for s_sub in ss:
        row = []
        for s in s_sub:
            p = jnp.exp(s - jnp.max(s, axis=-1, keepdims=True))
            row.append((p * (1.0 / jnp.sum(p, axis=-1, keepdims=True))).astype(BF16))
        ps.append(row)
    os = [jnp.concatenate([_dot(p, kv_ref[:, vc]).astype(BF16) for p, (_, vc) in zip(p_sub, heads)], axis=1)
          for p_sub in ps]
    cross = [_dot(o, wo_ref[...]) for o in os]
    for r, x, c in zip(subs, xs, cross):
        o_ref[r, :] = _layer_norm(ALPHA * x + c, g_ref[...], b_ref[...])


def _xattn(x3, kv3, wq, wo, g, b, tm):
    B, S, _ = x3.shape
    M = kv3.shape[1]
    return pl.pallas_call(
        _xattn_kernel,
        grid=(B, S // tm),
        in_specs=[pl.BlockSpec((None, tm, D_MODEL), lambda bi, i: (bi, i, 0)),
                  pl.BlockSpec((None, M, 2 * D_MODEL), lambda bi, i: (bi, 0, 0)),
                  _const_spec(wq.shape), _const_spec(wo.shape), _const_spec(g.shape), _const_spec(b.shape)],
        out_specs=pl.BlockSpec((None, tm, D_MODEL), lambda bi, i: (bi, i, 0)),
        out_shape=jax.ShapeDtypeStruct((B, S, D_MODEL), F32),
        compiler_params=_cparams(2),
        name="xattn",
    )(x3, kv3, wq, wo, g, b)


def _ffn_kernel(x_ref, w13_ref, w2_ref, g_ref, b_ref, o_ref):
    x = x_ref[...]
    xb = x.astype(BF16)
    acc = jnp.zeros(x.shape, F32)
    for j in range(D_FF // FF_CHUNK):
        ca = slice(j * FF_CHUNK, (j + 1) * FF_CHUNK)
        cb = slice(D_FF + j * FF_CHUNK, D_FF + (j + 1) * FF_CHUNK)
        a = _dot(xb, w13_ref[:, ca])
        h = (a * _sigmoid(a)) * _dot(xb, w13_ref[:, cb])
        acc = acc + _dot(h.astype(BF16), w2_ref[ca, :])
    o_ref[...] = _layer_norm(ALPHA * x + acc, g_ref[...], b_ref[...])


def _ffn(x2d, w13, w2, g, b, tm):
    T = x2d.shape[0]
    return pl.pallas_call(
        _ffn_kernel,
        grid=(T // tm,),
        in_specs=[pl.BlockSpec((tm, D_MODEL), lambda i: (i, 0)),
                  _const_spec(w13.shape), _const_spec(w2.shape), _const_spec(g.shape), _const_spec(b.shape)],
        out_specs=pl.BlockSpec((tm, D_MODEL), lambda i: (i, 0)),
        out_shape=jax.ShapeDtypeStruct((T, D_MODEL), F32),
        compiler_params=_cparams(1),
        name="ffn",
    )(x2d, w13, w2, g, b)


def _relayout_w_in(w):
    sizes = (3 * DN_W, DN_W, DN_HEADS, DN_HEADS, DIL_W, DIL_W, DIL_W, D_MODEL, D_MODEL)
    offs = [0]
    for sz in sizes:
        offs.append(offs[-1] + sz)
    wt = w.T
    part = lambda i: wt[offs[i]:offs[i + 1], :]
    used = COL_GATES + 2 * DN_HEADS
    pad = jnp.zeros((NP - used, w.shape[0]), w.dtype)
    rows = jnp.concatenate([part(0), part(1), part(7), part(8), part(4), part(5), part(6),
                            part(2), part(3), pad], axis=0)
    return rows.astype(BF16).T


def _pad_row(v, fill=0.0):
    return jnp.concatenate([v.astype(F32), jnp.full((LANES - v.shape[0],), fill, F32)])[None, :]


def kernel(x, mem, positions, w_in, conv_w, a_log, dt_bias, dn_norm_g, w_br_a, w_br_b, w_mix_out,
           ln1_g, ln1_b, w_xq, w_xkv, w_xo, ln2_g, ln2_b, w_ffn13, w_ffn2, ln3_g, ln3_b):
    B, S, D = x.shape
    T = B * S
    M = mem.shape[1]
    assert D == D_MODEL and w_in.shape[0] == 1 and S % (DIL_BLOCK * DIL_GROUPS[-1][1]) == 0
    x2d = x.reshape(T, D)
    bf = lambda w: w[0].astype(BF16)
    row = lambda v: v[0][None, :].astype(F32)

    P, G = _in_proj(x2d, _relayout_w_in(w_in[0]), conv_w[0], tm=S)
    P3 = P.reshape(B, S, NP)

    o_a = _deltanet(P3, G.reshape(B, S, LANES), _pad_row(a_log[0]), _pad_row(dt_bias[0]), row(dn_norm_g))

    inv_freq = ROPE_THETA ** (-jnp.arange(0, HEAD_DIM, 2, dtype=F32) / HEAD_DIM)
    invf_row = jnp.concatenate([inv_freq, inv_freq])[None, :]
    o_b = _dilated(P3, positions.reshape(B, S, 1), invf_row)

    x1 = _mix(o_a.reshape(T, DN_W), o_b.reshape(T, DIL_OUT), P, x2d, bf(w_br_a), bf(w_br_b), bf(w_mix_out),
              row(ln1_g), row(ln1_b), tm=2 * ROW_SUB)

    kv = _mem_kv(mem.reshape(B * M, D), bf(w_xkv), tm=512)
    x2 = _xattn(x1.reshape(B, S, D), kv.reshape(B, M, 2 * D), bf(w_xq), bf(w_xo), row(ln2_g), row(ln2_b), tm=2 * ROW_SUB)

    out = _ffn(x2.reshape(T, D), bf(w_ffn13), bf(w_ffn2), row(ln3_g), row(ln3_b), tm=512)
    return out.reshape(B, S, D)
```

```python
import functools
import math

import jax
import jax.numpy as jnp
from jax import lax
from jax.experimental import pallas as pl
from jax.experimental.pallas import tpu as pltpu

F32 = jnp.float32
BF16 = jnp.bfloat16

LANES = 128
D_MODEL = 1024
DN_HEADS = 8
HEAD_DIM = 128
DN_W = DN_HEADS * HEAD_DIM
DN_CONV = 4
DN_BLOCK = 128
DN_HPG = 4
INV_BASE = 16
DN_PART_BLOCKS = (2, 5, 5, 4)
DIL_GROUPS = ((128, 1), (512, 4), (2048, 16))
DIL_HPG = 4
DIL_W = len(DIL_GROUPS) * DIL_HPG * HEAD_DIM
DIL_OUT = DIL_HPG * HEAD_DIM
DIL_BLOCK = 128
ROPE_THETA = 10000.0
MEM_HEADS = 4
MEM_HEAD_DIM = D_MODEL // MEM_HEADS
D_FF = 2816
FF_CHUNK = 256
ROW_SUB = 512
ALPHA = 2.0 ** 0.25
LN_EPS = 1e-5
RMS_EPS = 1e-6
LOG2_E = 1.4426950408889634

COL_QKV = 0
COL_Z = 3 * DN_W
COL_GA = COL_Z + DN_W
COL_GB = COL_GA + D_MODEL
COL_BQ = COL_GB + D_MODEL
COL_BK = COL_BQ + DIL_W
COL_BV = COL_BK + DIL_W
COL_GATES = COL_BV + DIL_W
PROJ_TN = 1024
PROJ_CONV_ROWS = 128
NP = 11264

VMEM_LIMIT = 52 * 1024 * 1024
DN_VMEM_LIMIT = 58 * 1024 * 1024


def _cparams(n_axes, vmem_limit=VMEM_LIMIT):
    return pltpu.CompilerParams(dimension_semantics=("arbitrary",) * n_axes,
                                vmem_limit_bytes=vmem_limit)


def _dot(a, b):
    return jnp.dot(a, b, preferred_element_type=F32)


def _dot_nt(a, b):
    return lax.dot_general(a, b, (((1,), (1,)), ((), ())), preferred_element_type=F32)


def _sigmoid(x):
    return 1.0 / (1.0 + jnp.exp2(x * -LOG2_E))


def _silu(x):
    return x * _sigmoid(x)


def _layer_norm(y, g, b):
    mu = jnp.mean(y, axis=-1, keepdims=True)
    yc = y - mu
    var = jnp.mean(yc * yc, axis=-1, keepdims=True)
    return yc * lax.rsqrt(var + LN_EPS) * g + b


def _proj_kernel(x_ref, w_ref, cw_ref, o_ref, gates_ref, xb_s):
    S = x_ref.shape[0]
    j = pl.program_id(1)
    n_conv = 3 * DN_W // PROJ_TN

    @pl.when(j == 0)
    def _():
        xb_s[...] = x_ref[...].astype(BF16)

    @pl.when(j < n_conv)
    def _():
        halo = 8
        w = w_ref[...]
        cw = cw_ref[...]
        tail = jnp.zeros((halo, PROJ_TN), F32)
        n_chunks = S // PROJ_CONV_ROWS
        chunk = lambda c: _dot(xb_s[c * PROJ_CONV_ROWS:(c + 1) * PROJ_CONV_ROWS, :], w)
        nxt = chunk(0)
        for c in range(n_chunks):
            rows = slice(c * PROJ_CONV_ROWS, (c + 1) * PROJ_CONV_ROWS)
            acc = nxt
            if c + 1 < n_chunks:
                nxt = chunk(c + 1)
            win = jnp.concatenate([tail, acc], axis=0)
            tail = acc[PROJ_CONV_ROWS - halo:, :]
            y = acc * cw[DN_CONV - 1:DN_CONV, :]
            for t in range(DN_CONV - 1):
                y = y + pltpu.roll(win, DN_CONV - 1 - t, 0)[halo:, :] * cw[t:t + 1, :]
            o_ref[rows, :] = _silu(y).astype(o_ref.dtype)

    @pl.when(j == COL_Z // PROJ_TN)
    def _():
        w = w_ref[...]
        n_chunks = S // PROJ_CONV_ROWS
        chunk = lambda c: _dot(xb_s[c * PROJ_CONV_ROWS:(c + 1) * PROJ_CONV_ROWS, :], w)
        nxt = chunk(0)
        for c in range(n_chunks):
            acc = nxt
            if c + 1 < n_chunks:
                nxt = chunk(c + 1)
            o_ref[c * PROJ_CONV_ROWS:(c + 1) * PROJ_CONV_ROWS, :] = _silu(acc).astype(o_ref.dtype)

    @pl.when(j > COL_Z // PROJ_TN)
    def _():
        acc = _dot(xb_s[...], w_ref[...])
        o_ref[...] = acc.astype(o_ref.dtype)

        @pl.when(j == COL_GATES // PROJ_TN)
        def _():
            lo = COL_GATES % PROJ_TN
            gates_ref[...] = acc[:, lo:lo + LANES]


def _in_proj(x2d, wb, conv_w, tm):
    T = x2d.shape[0]
    n_conv = 3 * DN_W // PROJ_TN
    assert COL_QKV == 0 and COL_Z == 3 * DN_W and DN_W % PROJ_TN == 0
    return pl.pallas_call(
        _proj_kernel,
        grid=(T // tm, NP // PROJ_TN),
        in_specs=[pl.BlockSpec((tm, D_MODEL), lambda i, j: (i, 0)),
                  pl.BlockSpec((D_MODEL, PROJ_TN), lambda i, j: (0, j)),
                  pl.BlockSpec((DN_CONV, PROJ_TN), lambda i, j: (0, jnp.minimum(j, n_conv - 1)))],
        out_specs=[pl.BlockSpec((tm, PROJ_TN), lambda i, j: (i, j)),
                   pl.BlockSpec((tm, LANES), lambda i, j: (i, 0))],
        out_shape=[jax.ShapeDtypeStruct((T, NP), BF16), jax.ShapeDtypeStruct((T, LANES), F32)],
        scratch_shapes=[pltpu.VMEM((tm, D_MODEL), BF16)],
        compiler_params=_cparams(2),
        name="in_proj",
    )(x2d, wb, conv_w)


def _l2norm(x):
    return x * lax.rsqrt(jnp.sum(x * x, axis=-1, keepdims=True) + RMS_EPS)


def _deltanet_kernel(q_ref, k_ref, v_ref, z_ref, gates_ref, alog_ref, dtb_ref, ng_ref, o_ref,
                     gam_s, beta_s, n_s, t_s, p_s, qk_s, rhs_s, mq_s, kdt_s, qd_s, gt_s):
    S = q_ref.shape[0]
    nblk = S // DN_BLOCK
    hg = pl.program_id(1)
    blk = DN_BLOCK
    bc_s, o0_s = n_s, t_s

    @pl.when(hg == 0)
    def _():
        G = gates_ref[...]
        zz = G + dtb_ref[...]
        softplus = jnp.maximum(zz, 0.0) + jnp.log1p(jnp.exp(-jnp.abs(zz)))
        gam = -jnp.exp(alog_ref[...]) * softplus
        rowc = lax.broadcasted_iota(jnp.int32, G.shape, 0) & (blk - 1)
        sh = 1
        while sh < blk:
            gam = gam + jnp.where(rowc >= sh, pltpu.roll(gam, sh, 0), 0.0)
            sh *= 2
        gam_s[...] = gam
        beta_s[...] = _sigmoid(G)

    ri = lax.broadcasted_iota(jnp.int32, (blk, blk), 0)
    ci = lax.broadcasted_iota(jnp.int32, (blk, blk), 1)

    def same_block(b):
        s = int(math.log2(b))
        return (ri >> s) == (ci >> s)

    def setup(c, carry=None):
        rows = pl.ds(pl.multiple_of(c * blk, blk), blk)
        gsl = gam_s[rows, :]
        bsl = beta_s[rows, :]
        for i in range(DN_HPG):
            h = hg * DN_HPG + i
            cols = slice(i * HEAD_DIM, (i + 1) * HEAD_DIM)
            gam_col = jnp.sum(jnp.where(ci == h, gsl, 0.0), axis=1, keepdims=True)
            beta_col = jnp.sum(jnp.where(ci == h + DN_HEADS, bsl, 0.0), axis=1, keepdims=True)
            gb = jnp.broadcast_to(gam_col, (blk, HEAD_DIM))
            kn = _l2norm(k_ref[rows, cols].astype(F32))
            qn = _l2norm(q_ref[rows, cols].astype(F32)) * (HEAD_DIM ** -0.5)
            kb = kn * beta_col
            dec = jnp.exp(jnp.where(ri >= ci, gb - gb.T, -jnp.inf))
            a = _dot_nt(jnp.concatenate([qn, kb], axis=0).astype(BF16), kn.astype(BF16))
            qk_s[i, c] = (a[:blk] * dec).astype(BF16)
            n_s[i, c] = -a[blk:] * jnp.where(ri > ci, dec, 0.0)
            eg = jnp.exp(gb)
            rhs_s[i, c] = jnp.concatenate([v_ref[rows, cols].astype(F32) * beta_col, kb * eg], axis=1).astype(BF16)
            qd_s[i, c] = (qn * eg).astype(BF16)
            gl = jnp.broadcast_to(gb[blk - 1:blk, :], (blk, HEAD_DIM))
            kdt_s[i, c] = (kn * jnp.exp(gl - gb)).T.astype(BF16)
            gt_s[i, c] = jnp.exp(gl[:8, :])
        return carry

    def series_start(i, c):
        d = jnp.where(same_block(INV_BASE), n_s[i, c], 0.0)
        return d, _dot(d.astype(BF16), d.astype(BF16))

    def series_start_done(i, c, r):
        t_s[i, c] = (ri == ci).astype(F32) + r[0]
        p_s[i, c] = r[1].astype(BF16)

    def series_both(i, c):
        pb = p_s[i, c]
        return _dot(jnp.concatenate([t_s[i, c].astype(BF16), pb], axis=0), pb)

    def series_both_done(i, c, r):
        t_s[i, c] = t_s[i, c] + r[:blk]
        p_s[i, c] = r[blk:].astype(BF16)

    def times_p(i, c):
        return _dot(t_s[i, c].astype(BF16), p_s[i, c])

    def add_to_t(i, c, r):
        t_s[i, c] = t_s[i, c] + r

    def to_p(i, c, r):
        p_s[i, c] = r.astype(BF16)

    stages = [(series_start, series_start_done)]
    stages += [(series_both, series_both_done)] * (int(math.log2(INV_BASE)) - 2)
    stages += [(times_p, add_to_t)]
    b = INV_BASE
    while b < blk:
        off = same_block(2 * b) & jnp.logical_not(same_block(b))

        def off_times_t(i, c, off=off):
            return _dot(jnp.where(off, n_s[i, c], 0.0).astype(BF16), t_s[i, c].astype(BF16))

        stages += [(off_times_t, to_p), (times_p, add_to_t)]
        b *= 2

    def solve(i, c):
        return _dot(t_s[i, c].astype(BF16), rhs_s[i, c])

    def solve_done(i, c, r):
        rhs_s[i, c] = r.astype(BF16)

    def cross(i, c):
        return _dot(jnp.concatenate([kdt_s[i, c], qk_s[i, c]], axis=0), rhs_s[i, c])

    def cross_done(i, c, r):
        bc_s[i, c] = r[:blk, :HEAD_DIM]
        o0_s[i, c] = r[blk:, :HEAD_DIM]
        mq_s[i, c] = jnp.concatenate([r[:blk, HEAD_DIM:], qd_s[i, c].astype(F32) - r[blk:, HEAD_DIM:]],
                                     axis=0).astype(BF16)

    stages += [(solve, solve_done), (cross, cross_done)]

    def run_stages(blocks, side_work):
        probs = [(i, c) for c in blocks for i in range(DN_HPG)]
        for k, (issue, finish) in enumerate(stages):
            results = [issue(i, c) for i, c in probs]
            if k < len(side_work):
                side_work[k]()
            for (i, c), r in zip(probs, results):
                finish(i, c, r)

    ng = ng_ref[...]

    def scan_step(c, states):
        rows = pl.ds(pl.multiple_of(c * blk, blk), blk)
        new_states = []
        for i in range(DN_HPG):
            cols = slice(i * HEAD_DIM, (i + 1) * HEAD_DIM)
            st = states[i]
            r = _dot(mq_s[i, c], st.astype(BF16))
            g_tot = jnp.broadcast_to(gt_s[i, c][0:1, :], (HEAD_DIM, HEAD_DIM))
            new_states.append(g_tot * st + bc_s[i, c] - r[:blk])
            o = o0_s[i, c] + r[blk:]
            o = o * lax.rsqrt(jnp.mean(o * o, axis=-1, keepdims=True) + RMS_EPS) * ng
            o_ref[rows, cols] = (o * z_ref[rows, cols].astype(F32)).astype(o_ref.dtype)
        return tuple(new_states)

    assert sum(DN_PART_BLOCKS) == nblk
    starts = [sum(DN_PART_BLOCKS[:p]) for p in range(len(DN_PART_BLOCKS) + 1)]
    parts = [list(range(starts[p], starts[p + 1])) for p in range(len(DN_PART_BLOCKS))]
    lax.fori_loop(0, len(parts[0]), setup, 0)

    state = [tuple(jnp.zeros((HEAD_DIM, HEAD_DIM), F32) for _ in range(DN_HPG))]

    def scan_side(c):
        state[0] = scan_step(c, state[0])

    for p in range(len(parts)):
        side = [functools.partial(setup, c) for c in (parts[p + 1] if p + 1 < len(parts) else [])]
        side += [functools.partial(scan_side, c) for c in (parts[p - 1] if p > 0 else [])]
        assert len(side) <= len(stages)
        run_stages(parts[p], side)
    lax.fori_loop(parts[-1][0], nblk, scan_step, state[0])


def _deltanet(P3, G3, alog_row, dtb_row, ng_row):
    B, S, _ = P3.shape
    W = DN_HPG * HEAD_DIM
    ngrp = DN_HEADS // DN_HPG
    nblk = S // DN_BLOCK

    def col_spec(base):
        return pl.BlockSpec((None, S, W), lambda b, g: (b, 0, base // W + g))

    row_spec = pl.BlockSpec((1, LANES), lambda b, g: (0, 0))
    blk_buf = lambda rows, cols, dt: pltpu.VMEM((DN_HPG, nblk, rows, cols), dt)
    return pl.pallas_call(
        _deltanet_kernel,
        grid=(B, ngrp),
        in_specs=[col_spec(COL_QKV), col_spec(COL_QKV + DN_W), col_spec(COL_QKV + 2 * DN_W), col_spec(COL_Z),
                  pl.BlockSpec((None, S, LANES), lambda b, g: (b, 0, 0), pipeline_mode=pl.Buffered(1)),
                  row_spec, row_spec, row_spec],
        out_specs=pl.BlockSpec((None, S, W), lambda b, g: (b, 0, g)),
        out_shape=jax.ShapeDtypeStruct((B, S, DN_W), BF16),
        scratch_shapes=[pltpu.VMEM((S, LANES), F32), pltpu.VMEM((S, LANES), F32),
                        blk_buf(DN_BLOCK, DN_BLOCK, F32), blk_buf(DN_BLOCK, DN_BLOCK, F32),
                        blk_buf(DN_BLOCK, DN_BLOCK, BF16), blk_buf(DN_BLOCK, DN_BLOCK, BF16),
                        blk_buf(DN_BLOCK, 2 * HEAD_DIM, BF16),
                        blk_buf(2 * DN_BLOCK, HEAD_DIM, BF16),
                        blk_buf(HEAD_DIM, DN_BLOCK, BF16),
                        blk_buf(DN_BLOCK, HEAD_DIM, BF16),
                        blk_buf(8, HEAD_DIM, F32)],
        compiler_params=_cparams(2, DN_VMEM_LIMIT),
        name="deltanet",
    )(P3, P3, P3, P3, G3, alog_row, dtb_row, ng_row)


def _dilated_kernel(q0_ref, q1_ref, q2_ref, k0_ref, k1_ref, k2_ref, v0_ref, v1_ref, v2_ref,
                    pos_ref, invf_ref, o_ref, cos_s, sin_s, qr_s, kr_s, vr_s, og_s, lg_s):
    S = q0_ref.shape[0]
    blk = DIL_BLOCK
    qscale = HEAD_DIM ** -0.5 * LOG2_E

    @pl.when(pl.program_id(1) == 0)
    def _():
        hs, hl = S // 2, HEAD_DIM // 2
        pos = pos_ref[...].astype(F32)
        lane = lax.broadcasted_iota(jnp.int32, (hs, HEAD_DIM), 1)
        low = lane < hl
        ang = jnp.where(low, pos[:hs, :], pos[hs:, :]) * invf_ref[...]
        c, sn = jnp.cos(ang), jnp.sin(ang)
        c_sw, sn_sw = pltpu.roll(c, hl, 1), pltpu.roll(sn, hl, 1)
        cos_s[pl.ds(0, hs), :] = jnp.where(low, c, c_sw)
        cos_s[pl.ds(hs, hs), :] = jnp.where(low, c_sw, c)
        sin_s[pl.ds(0, hs), :] = jnp.where(low, -sn, sn_sw)
        sin_s[pl.ds(hs, hs), :] = jnp.where(low, -sn_sw, sn)

    def rope(x):
        return x * cos_s[...] + pltpu.roll(x, HEAD_DIM // 2, 1) * sin_s[...]

    ri = lax.broadcasted_iota(jnp.int32, (blk, blk), 0)
    ci = lax.broadcasted_iota(jnp.int32, (blk, blk), 1)
    neg = -jnp.inf

    def rows_of(start, dil):
        return pl.ds(start, blk) if dil == 1 else pl.ds(start, blk, stride=dil)

    ones = jnp.ones((blk, HEAD_DIM), BF16)

    def block_group(g, starts, dil, has_prev):
        rows = [rows_of(st, dil) for st in starts]
        q = [qr_s[g, r, :].astype(BF16) for r in rows]
        s = [_dot_nt(q[j], kr_s[g, r, :].astype(BF16)) for j, r in enumerate(rows)]
        if has_prev:
            prows = [rows_of(st - blk * dil, dil) for st in starts]
            sp = [_dot_nt(q[j], kr_s[g, r, :].astype(BF16)) for j, r in enumerate(prows)]
        ps, pps, ms = [], [], []
        for j in range(len(starts)):
            sj = jnp.where(ri >= ci, s[j], neg)
            if has_prev:
                spj = jnp.where(ci >= ri, sp[j], neg)
                m = jnp.max(jnp.maximum(sj, spj), axis=-1, keepdims=True)
                pps.append(jnp.exp2(spj - m).astype(BF16))
            else:
                m = jnp.max(sj, axis=-1, keepdims=True)
            ps.append(jnp.exp2(sj - m).astype(BF16))
            ms.append(m)
        for j, r in enumerate(rows):
            ol = _dot(ps[j], jnp.concatenate([vr_s[g, r, :].astype(BF16), ones], axis=1))
            if has_prev:
                ol = ol + _dot(pps[j], jnp.concatenate([vr_s[g, prows[j], :].astype(BF16), ones], axis=1))
            l = ol[:, HEAD_DIM:]
            og_s[g, r, :] = ol[:, :HEAD_DIM] * (1.0 / l)
            lg_s[g, r, :] = ms[j] * (1.0 / LOG2_E) + jnp.log(l)

    q_refs = (q0_ref, q1_ref, q2_ref)
    k_refs = (k0_ref, k1_ref, k2_ref)
    v_refs = (v0_ref, v1_ref, v2_ref)
    for g in range(len(DIL_GROUPS)):
        qr_s[g] = rope(q_refs[g][...].astype(F32)) * qscale
        kr_s[g] = rope(k_refs[g][...].astype(F32))
        vr_s[g] = v_refs[g][...].astype(F32)
    for g, (window, dil) in enumerate(DIL_GROUPS):
        assert window // dil == blk
        nb = S // dil // blk
        block_group(g, list(range(dil)), dil, False)
        if nb > 1:
            block_group(g, [n * blk * dil + r for n in range(1, nb) for r in range(dil)], dil, True)

    l0, l1, l2 = lg_s[0], lg_s[1], lg_s[2]
    m = jnp.maximum(jnp.maximum(l0, l1), l2)
    e0, e1, e2 = jnp.exp(l0 - m), jnp.exp(l1 - m), jnp.exp(l2 - m)
    inv = 1.0 / (e0 + e1 + e2)
    o_ref[...] = ((e0 * inv) * og_s[0] + (e1 * inv) * og_s[1] + (e2 * inv) * og_s[2]).astype(o_ref.dtype)


def _dilated(P3, pos3, invf_row):
    B, S, _ = P3.shape

    def head_spec(base, g):
        return pl.BlockSpec((None, S, HEAD_DIM), lambda b, h: (b, 0, base // HEAD_DIM + g * DIL_HPG + h))

    ngr = len(DIL_GROUPS)
    in_specs = ([head_spec(COL_BQ, g) for g in range(ngr)] + [head_spec(COL_BK, g) for g in range(ngr)]
                + [head_spec(COL_BV, g) for g in range(ngr)]
                + [pl.BlockSpec((None, S, 1), lambda b, h: (b, 0, 0)),
                   pl.BlockSpec((1, HEAD_DIM), lambda b, h: (0, 0))])
    big = pltpu.VMEM((S, HEAD_DIM), F32)
    return pl.pallas_call(
        _dilated_kernel,
        grid=(B, DIL_HPG),
        in_specs=in_specs,
        out_specs=pl.BlockSpec((None, S, HEAD_DIM), lambda b, h: (b, 0, h)),
        out_shape=jax.ShapeDtypeStruct((B, S, DIL_OUT), BF16),
        scratch_shapes=[big, big] + [pltpu.VMEM((ngr, S, HEAD_DIM), F32)] * 5,
        compiler_params=_cparams(2),
        name="dilated",
    )(*([P3] * 9), pos3, invf_row)


def _mix_kernel(oa_ref, ob_ref, ga_ref, gb_ref, x_ref, wa_ref, wb_ref, wm_ref, g_ref, b_ref, o_ref):
    subs = [slice(r, r + ROW_SUB) for r in range(0, x_ref.shape[0], ROW_SUB)]
    ya = [_dot(oa_ref[r, :], wa_ref[...]) for r in subs]
    yb = [_dot(ob_ref[r, :], wb_ref[...]) for r in subs]
    hs = [(_sigmoid(ga_ref[r, :].astype(F32)) * a + _sigmoid(gb_ref[r, :].astype(F32)) * b).astype(BF16)
          for r, a, b in zip(subs, ya, yb)]
    mix = [_dot(h, wm_ref[...]) for h in hs]
    for r, m in zip(subs, mix):
        o_ref[r, :] = _layer_norm(ALPHA * x_ref[r, :] + m, g_ref[...], b_ref[...])


def _const_spec(shape):
    return pl.BlockSpec(shape, lambda *_: (0,) * len(shape), pipeline_mode=pl.Buffered(1))


def _mix(oa, ob, P, x2d, wa, wb, wm, g, b, tm):
    T = x2d.shape[0]
    row = lambda w, j=0: pl.BlockSpec((tm, w), lambda i: (i, j))
    return pl.pallas_call(
        _mix_kernel,
        grid=(T // tm,),
        in_specs=[row(DN_W), row(DIL_OUT), row(D_MODEL, COL_GA // D_MODEL), row(D_MODEL, COL_GB // D_MODEL),
                  row(D_MODEL), _const_spec(wa.shape), _const_spec(wb.shape), _const_spec(wm.shape),
                  _const_spec(g.shape), _const_spec(b.shape)],
        out_specs=row(D_MODEL),
        out_shape=jax.ShapeDtypeStruct((T, D_MODEL), F32),
        compiler_params=_cparams(1),
        name="mix",
    )(oa, ob, P, P, x2d, wa, wb, wm, g, b)


def _memkv_kernel(m_ref, w_ref, o_ref):
    o_ref[...] = _dot(m_ref[...].astype(BF16), w_ref[...]).astype(o_ref.dtype)


def _mem_kv(mem2d, wkv, tm):
    R = mem2d.shape[0]
    return pl.pallas_call(
        _memkv_kernel,
        grid=(R // tm,),
        in_specs=[pl.BlockSpec((tm, D_MODEL), lambda i: (i, 0)), _const_spec(wkv.shape)],
        out_specs=pl.BlockSpec((tm, 2 * D_MODEL), lambda i: (i, 0)),
        out_shape=jax.ShapeDtypeStruct((R, 2 * D_MODEL), BF16),
        compiler_params=_cparams(1),
        name="mem_kv",
    )(mem2d, wkv)


def _xattn_kernel(x_ref, kv_ref, wq_ref, wo_ref, g_ref, b_ref, o_ref):
    tm = x_ref.shape[0]
    subs = [slice(r, r + ROW_SUB) for r in range(0, tm, ROW_SUB)]
    scale = MEM_HEAD_DIM ** -0.5
    heads = [(slice(h * MEM_HEAD_DIM, (h + 1) * MEM_HEAD_DIM),
              slice(D_MODEL + h * MEM_HEAD_DIM, D_MODEL + (h + 1) * MEM_HEAD_DIM)) for h in range(MEM_HEADS)]
    xs = [x_ref[r, :] for r in subs]
    qs = [_dot(x.astype(BF16), wq_ref[...]) for x in xs]
    ss = [[_dot_nt(q[:, kc].astype(BF16), kv_ref[:, kc]) * scale for kc, _ in heads] for q in qs]
    ps = []
    for s_sub in ss:
        row = []
        for s in s_sub:
            p = jnp.exp(s - jnp.max(s, axis=-1, keepdims=True))
            row.append((p * (1.0 / jnp.sum(p, axis=-1, keepdims=True))).astype(BF16))
        ps.append(row)
    os = [jnp.concatenate([_dot(p, kv_ref[:, vc]).astype(BF16) for p, (_, vc) in zip(p_sub, heads)], axis=1)
          for p_sub in ps]
    cross = [_dot(o, wo_ref[...]) for o in os]
    for r, x, c in zip(subs, xs, cross):
        o_ref[r, :] = _layer_norm(ALPHA * x + c, g_ref[...], b_ref[...])


def _xattn(x3, kv3, wq, wo, g, b, tm):
    B, S, _ = x3.shape
    M = kv3.shape[1]
    return pl.pallas_call(
        _xattn_kernel,
        grid=(B, S // tm),
        in_specs=[pl.BlockSpec((None, tm, D_MODEL), lambda bi, i: (bi, i, 0)),
                  pl.BlockSpec((None, M, 2 * D_MODEL), lambda bi, i: (bi, 0, 0)),
                  _const_spec(wq.shape), _const_spec(wo.shape), _const_spec(g.shape), _const_spec(b.shape)],
        out_specs=pl.BlockSpec((None, tm, D_MODEL), lambda bi, i: (bi, i, 0)),
        out_shape=jax.ShapeDtypeStruct((B, S, D_MODEL), F32),
        compiler_params=_cparams(2),
        name="xattn",
    )(x3, kv3, wq, wo, g, b)


def _ffn_kernel(x_ref, w13_ref, w2_ref, g_ref, b_ref, o_ref):
    x = x_ref[...]
    xb = x.astype(BF16)
    acc = jnp.zeros(x.shape, F32)
    for j in range(D_FF // FF_CHUNK):
        ca = slice(j * FF_CHUNK, (j + 1) * FF_CHUNK)
        cb = slice(D_FF + j * FF_CHUNK, D_FF + (j + 1) * FF_CHUNK)
        a = _dot(xb, w13_ref[:, ca])
        h = (a * _sigmoid(a)) * _dot(xb, w13_ref[:, cb])
        acc = acc + _dot(h.astype(BF16), w2_ref[ca, :])
    o_ref[...] = _layer_norm(ALPHA * x + acc, g_ref[...], b_ref[...])


def _ffn(x2d, w13, w2, g, b, tm):
    T = x2d.shape[0]
    return pl.pallas_call(
        _ffn_kernel,
        grid=(T // tm,),
        in_specs=[pl.BlockSpec((tm, D_MODEL), lambda i: (i, 0)),
                  _const_spec(w13.shape), _const_spec(w2.shape), _const_spec(g.shape), _const_spec(b.shape)],
        out_specs=pl.BlockSpec((tm, D_MODEL), lambda i: (i, 0)),
        out_shape=jax.ShapeDtypeStruct((T, D_MODEL), F32),
        compiler_params=_cparams(1),
        name="ffn",
    )(x2d, w13, w2, g, b)


def _relayout_w_in(w):
    sizes = (3 * DN_W, DN_W, DN_HEADS, DN_HEADS, DIL_W, DIL_W, DIL_W, D_MODEL, D_MODEL)
    offs = [0]
    for sz in sizes:
        offs.append(offs[-1] + sz)
    wt = w.T
    part = lambda i: wt[offs[i]:offs[i + 1], :]
    used = COL_GATES + 2 * DN_HEADS
    pad = jnp.zeros((NP - used, w.shape[0]), w.dtype)
    rows = jnp.concatenate([part(0), part(1), part(7), part(8), part(4), part(5), part(6),
                            part(2), part(3), pad], axis=0)
    return rows.astype(BF16).T


def _pad_row(v, fill=0.0):
    return jnp.concatenate([v.astype(F32), jnp.full((LANES - v.shape[0],), fill, F32)])[None, :]


def kernel(x, mem, positions, w_in, conv_w, a_log, dt_bias, dn_norm_g, w_br_a, w_br_b, w_mix_out,
           ln1_g, ln1_b, w_xq, w_xkv, w_xo, ln2_g, ln2_b, w_ffn13, w_ffn2, ln3_g, ln3_b):
    B, S, D = x.shape
    T = B * S
    M = mem.shape[1]
    assert D == D_MODEL and w_in.shape[0] == 1 and S % (DIL_BLOCK * DIL_GROUPS[-1][1]) == 0
    x2d = x.reshape(T, D)
    bf = lambda w: w[0].astype(BF16)
    row = lambda v: v[0][None, :].astype(F32)

    P, G = _in_proj(x2d, _relayout_w_in(w_in[0]), conv_w[0], tm=S)
    P3 = P.reshape(B, S, NP)

    o_a = _deltanet(P3, G.reshape(B, S, LANES), _pad_row(a_log[0]), _pad_row(dt_bias[0]), row(dn_norm_g))

    inv_freq = ROPE_THETA ** (-jnp.arange(0, HEAD_DIM, 2, dtype=F32) / HEAD_DIM)
    invf_row = jnp.concatenate([inv_freq, inv_freq])[None, :]
    o_b = _dilated(P3, positions.reshape(B, S, 1), invf_row)

    x1 = _mix(o_a.reshape(T, DN_W), o_b.reshape(T, DIL_OUT), P, x2d, bf(w_br_a), bf(w_br_b), bf(w_mix_out),
              row(ln1_g), row(ln1_b), tm=2 * ROW_SUB)

    kv = _mem_kv(mem.reshape(B * M, D), bf(w_xkv), tm=512)
    x2 = _xattn(x1.reshape(B, S, D), kv.reshape(B, M, 2 * D), bf(w_xq), bf(w_xo), row(ln2_g), row(ln2_b), tm=2 * ROW_SUB)

    out = _ffn(x2.reshape(T, D), bf(w_ffn13), bf(w_ffn2), row(ln3_g), row(ln3_b), tm=512)
    return out.reshape(B, S, D)
```

```python
import functools
import math

import jax
import jax.numpy as jnp
from jax import lax
from jax.experimental import pallas as pl
from jax.experimental.pallas import tpu as pltpu

F32 = jnp.float32
BF16 = jnp.bfloat16

LANES = 128
D_MODEL = 1024
DN_HEADS = 8
HEAD_DIM = 128
DN_W = DN_HEADS * HEAD_DIM
DN_CONV = 4
DN_BLOCK = 128
DN_HPG = 4
INV_BASE = 16
DN_PART_BLOCKS = (2, 5, 5, 4)
DIL_GROUPS = ((128, 1), (512, 4), (2048, 16))
DIL_HPG = 4
DIL_W = len(DIL_GROUPS) * DIL_HPG * HEAD_DIM
DIL_OUT = DIL_HPG * HEAD_DIM
DIL_BLOCK = 128
ROPE_THETA = 10000.0
MEM_HEADS = 4
MEM_HEAD_DIM = D_MODEL // MEM_HEADS
D_FF = 2816
FF_CHUNK = 256
ROW_SUB = 512
ALPHA = 2.0 ** 0.25
LN_EPS = 1e-5
RMS_EPS = 1e-6
LOG2_E = 1.4426950408889634

COL_QKV = 0
COL_Z = 3 * DN_W
COL_GA = COL_Z + DN_W
COL_GB = COL_GA + D_MODEL
COL_BQ = COL_GB + D_MODEL
COL_BK = COL_BQ + DIL_W
COL_BV = COL_BK + DIL_W
COL_GATES = COL_BV + DIL_W
PROJ_TN = 1024
PROJ_CONV_ROWS = 256
NP = 11264

VMEM_LIMIT = 52 * 1024 * 1024
DN_VMEM_LIMIT = 58 * 1024 * 1024


def _cparams(n_axes, vmem_limit=VMEM_LIMIT):
    return pltpu.CompilerParams(dimension_semantics=("arbitrary",) * n_axes,
                                vmem_limit_bytes=vmem_limit)


def _dot(a, b):
    return jnp.dot(a, b, preferred_element_type=F32)


def _dot_nt(a, b):
    return lax.dot_general(a, b, (((1,), (1,)), ((), ())), preferred_element_type=F32)


def _sigmoid(x):
    return 1.0 / (1.0 + jnp.exp2(x * -LOG2_E))


def _silu(x):
    return x * _sigmoid(x)


def _layer_norm(y, g, b):
    mu = jnp.mean(y, axis=-1, keepdims=True)
    yc = y - mu
    var = jnp.mean(yc * yc, axis=-1, keepdims=True)
    return yc * lax.rsqrt(var + LN_EPS) * g + b


def _proj_kernel(x_ref, w_ref, cw_ref, o_ref, gates_ref, xb_s):
    S = x_ref.shape[0]
    j = pl.program_id(1)
    n_conv = 3 * DN_W // PROJ_TN

    @pl.when(j == 0)
    def _():
        xb_s[...] = x_ref[...].astype(BF16)

    @pl.when(j < n_conv)
    def _():
        halo = 8
        w = w_ref[...]
        cw = cw_ref[...]
        tail = jnp.zeros((halo, PROJ_TN), F32)
        n_chunks = S // PROJ_CONV_ROWS
        chunk = lambda c: _dot(xb_s[c * PROJ_CONV_ROWS:(c + 1) * PROJ_CONV_ROWS, :], w)
        nxt = chunk(0)
        for c in range(n_chunks):
            rows = slice(c * PROJ_CONV_ROWS, (c + 1) * PROJ_CONV_ROWS)
            acc = nxt
            if c + 1 < n_chunks:
                nxt = chunk(c + 1)
            win = jnp.concatenate([tail, acc], axis=0)
            tail = acc[PROJ_CONV_ROWS - halo:, :]
            y = acc * cw[DN_CONV - 1:DN_CONV, :]
            for t in range(DN_CONV - 1):
                y = y + pltpu.roll(win, DN_CONV - 1 - t, 0)[halo:, :] * cw[t:t + 1, :]
            o_ref[rows, :] = _silu(y).astype(o_ref.dtype)

    @pl.when(j == COL_Z // PROJ_TN)
    def _():
        w = w_ref[...]
        n_chunks = S // PROJ_CONV_ROWS
        chunk = lambda c: _dot(xb_s[c * PROJ_CONV_ROWS:(c + 1) * PROJ_CONV_ROWS, :], w)
        nxt = chunk(0)
        for c in range(n_chunks):
            acc = nxt
            if c + 1 < n_chunks:
                nxt = chunk(c + 1)
            o_ref[c * PROJ_CONV_ROWS:(c + 1) * PROJ_CONV_ROWS, :] = _silu(acc).astype(o_ref.dtype)

    @pl.when(j > COL_Z // PROJ_TN)
    def _():
        acc = _dot(xb_s[...], w_ref[...])
        o_ref[...] = acc.astype(o_ref.dtype)

        @pl.when(j == COL_GATES // PROJ_TN)
        def _():
            lo = COL_GATES % PROJ_TN
            gates_ref[...] = acc[:, lo:lo + LANES]


def _in_proj(x2d, wb, conv_w, tm):
    T = x2d.shape[0]
    n_conv = 3 * DN_W // PROJ_TN
    assert COL_QKV == 0 and COL_Z == 3 * DN_W and DN_W % PROJ_TN == 0
    return pl.pallas_call(
        _proj_kernel,
        grid=(T // tm, NP // PROJ_TN),
        in_specs=[pl.BlockSpec((tm, D_MODEL), lambda i, j: (i, 0)),
                  pl.BlockSpec((D_MODEL, PROJ_TN), lambda i, j: (0, j)),
                  pl.BlockSpec((DN_CONV, PROJ_TN), lambda i, j: (0, jnp.minimum(j, n_conv - 1)))],
        out_specs=[pl.BlockSpec((tm, PROJ_TN), lambda i, j: (i, j)),
                   pl.BlockSpec((tm, LANES), lambda i, j: (i, 0))],
        out_shape=[jax.ShapeDtypeStruct((T, NP), BF16), jax.ShapeDtypeStruct((T, LANES), F32)],
        scratch_shapes=[pltpu.VMEM((tm, D_MODEL), BF16)],
        compiler_params=_cparams(2),
        name="in_proj",
    )(x2d, wb, conv_w)


def _l2norm(x):
    return x * lax.rsqrt(jnp.sum(x * x, axis=-1, keepdims=True) + RMS_EPS)


def _deltanet_kernel(q_ref, k_ref, v_ref, z_ref, gates_ref, alog_ref, dtb_ref, ng_ref, o_ref,
                     gam_s, beta_s, n_s, t_s, p_s, qk_s, rhs_s, mq_s, kdt_s, qd_s, gt_s):
    S = q_ref.shape[0]
    nblk = S // DN_BLOCK
    hg = pl.program_id(1)
    blk = DN_BLOCK
    bc_s, o0_s = n_s, t_s

    @pl.when(hg == 0)
    def _():
        G = gates_ref[...]
        zz = G + dtb_ref[...]
        softplus = jnp.maximum(zz, 0.0) + jnp.log1p(jnp.exp(-jnp.abs(zz)))
        gam = -jnp.exp(alog_ref[...]) * softplus
        rowc = lax.broadcasted_iota(jnp.int32, G.shape, 0) & (blk - 1)
        sh = 1
        while sh < blk:
            gam = gam + jnp.where(rowc >= sh, pltpu.roll(gam, sh, 0), 0.0)
            sh *= 2
        gam_s[...] = gam
        beta_s[...] = _sigmoid(G)

    ri = lax.broadcasted_iota(jnp.int32, (blk, blk), 0)
    ci = lax.broadcasted_iota(jnp.int32, (blk, blk), 1)

    def same_block(b):
        s = int(math.log2(b))
        return (ri >> s) == (ci >> s)

    def setup(c, carry=None):
        rows = pl.ds(pl.multiple_of(c * blk, blk), blk)
        gsl = gam_s[rows, :]
        bsl = beta_s[rows, :]
        for i in range(DN_HPG):
            h = hg * DN_HPG + i
            cols = slice(i * HEAD_DIM, (i + 1) * HEAD_DIM)
            gam_col = jnp.sum(jnp.where(ci == h, gsl, 0.0), axis=1, keepdims=True)
            beta_col = jnp.sum(jnp.where(ci == h + DN_HEADS, bsl, 0.0), axis=1, keepdims=True)
            gb = jnp.broadcast_to(gam_col, (blk, HEAD_DIM))
            kn = _l2norm(k_ref[rows, cols].astype(F32))
            qn = _l2norm(q_ref[rows, cols].astype(F32)) * (HEAD_DIM ** -0.5)
            kb = kn * beta_col
            dec = jnp.exp(jnp.where(ri >= ci, gb - gb.T, -jnp.inf))
            a = _dot_nt(jnp.concatenate([qn, kb], axis=0).astype(BF16), kn.astype(BF16))
            qk_s[i, c] = (a[:blk] * dec).astype(BF16)
            n_s[i, c] = -a[blk:] * jnp.where(ri > ci, dec, 0.0)
            eg = jnp.exp(gb)
            rhs_s[i, c] = jnp.concatenate([v_ref[rows, cols].astype(F32) * beta_col, kb * eg], axis=1).astype(BF16)
            qd_s[i, c] = (qn * eg).astype(BF16)
            gl = jnp.broadcast_to(gb[blk - 1:blk, :], (blk, HEAD_DIM))
            kdt_s[i, c] = (kn * jnp.exp(gl - gb)).T.astype(BF16)
            gt_s[i, c] = jnp.exp(gl[:8, :])
        return carry

    def series_start(i, c):
        d = jnp.where(same_block(INV_BASE), n_s[i, c], 0.0)
        return d, _dot(d.astype(BF16), d.astype(BF16))

    def series_start_done(i, c, r):
        t_s[i, c] = (ri == ci).astype(F32) + r[0]
        p_s[i, c] = r[1].astype(BF16)

    def series_both(i, c):
        pb = p_s[i, c]
        return _dot(jnp.concatenate([t_s[i, c].astype(BF16), pb], axis=0), pb)

    def series_both_done(i, c, r):
        t_s[i, c] = t_s[i, c] + r[:blk]
        p_s[i, c] = r[blk:].astype(BF16)

    def times_p(i, c):
        return _dot(t_s[i, c].astype(BF16), p_s[i, c])

    def add_to_t(i, c, r):
        t_s[i, c] = t_s[i, c] + r

    def to_p(i, c, r):
        p_s[i, c] = r.astype(BF16)

    stages = [(series_start, series_start_done)]
    stages += [(series_both, series_both_done)] * (int(math.log2(INV_BASE)) - 2)
    stages += [(times_p, add_to_t)]
    b = INV_BASE
    while b < blk:
        off = same_block(2 * b) & jnp.logical_not(same_block(b))

        def off_times_t(i, c, off=off):
            return _dot(jnp.where(off, n_s[i, c], 0.0).astype(BF16), t_s[i, c].astype(BF16))

        stages += [(off_times_t, to_p), (times_p, add_to_t)]
        b *= 2

    def solve(i, c):
        return _dot(t_s[i, c].astype(BF16), rhs_s[i, c])

    def solve_done(i, c, r):
        rhs_s[i, c] = r.astype(BF16)

    def cross(i, c):
        return _dot(jnp.concatenate([kdt_s[i, c], qk_s[i, c]], axis=0), rhs_s[i, c])

    def cross_done(i, c, r):
        bc_s[i, c] = r[:blk, :HEAD_DIM]
        o0_s[i, c] = r[blk:, :HEAD_DIM]
        mq_s[i, c] = jnp.concatenate([r[:blk, HEAD_DIM:], qd_s[i, c].astype(F32) - r[blk:, HEAD_DIM:]],
                                     axis=0).astype(BF16)

    stages += [(solve, solve_done), (cross, cross_done)]

    def run_stages(blocks, side_work):
        probs = [(i, c) for c in blocks for i in range(DN_HPG)]
        for k, (issue, finish) in enumerate(stages):
            results = [issue(i, c) for i, c in probs]
            if k < len(side_work):
                side_work[k]()
            for (i, c), r in zip(probs, results):
                finish(i, c, r)

    ng = ng_ref[...]

    def scan_step(c, states):
        rows = pl.ds(pl.multiple_of(c * blk, blk), blk)
        new_states = []
        for i in range(DN_HPG):
            cols = slice(i * HEAD_DIM, (i + 1) * HEAD_DIM)
            st = states[i]
            r = _dot(mq_s[i, c], st.astype(BF16))
            g_tot = jnp.broadcast_to(gt_s[i, c][0:1, :], (HEAD_DIM, HEAD_DIM))
            new_states.append(g_tot * st + bc_s[i, c] - r[:blk])
            o = o0_s[i, c] + r[blk:]
            o = o * lax.rsqrt(jnp.mean(o * o, axis=-1, keepdims=True) + RMS_EPS) * ng
            o_ref[rows, cols] = (o * z_ref[rows, cols].astype(F32)).astype(o_ref.dtype)
        return tuple(new_states)

    assert sum(DN_PART_BLOCKS) == nblk
    starts = [sum(DN_PART_BLOCKS[:p]) for p in range(len(DN_PART_BLOCKS) + 1)]
    parts = [list(range(starts[p], starts[p + 1])) for p in range(len(DN_PART_BLOCKS))]
    lax.fori_loop(0, len(parts[0]), setup, 0)

    state = [tuple(jnp.zeros((HEAD_DIM, HEAD_DIM), F32) for _ in range(DN_HPG))]

    def scan_side(c):
        state[0] = scan_step(c, state[0])

    for p in range(len(parts)):
        side = [functools.partial(setup, c) for c in (parts[p + 1] if p + 1 < len(parts) else [])]
        side += [functools.partial(scan_side, c) for c in (parts[p - 1] if p > 0 else [])]
        assert len(side) <= len(stages)
        run_stages(parts[p], side)
    lax.fori_loop(parts[-1][0], nblk, scan_step, state[0])


def _deltanet(P3, G3, alog_row, dtb_row, ng_row):
    B, S, _ = P3.shape
    W = DN_HPG * HEAD_DIM
    ngrp = DN_HEADS // DN_HPG
    nblk = S // DN_BLOCK

    def col_spec(base):
        return pl.BlockSpec((None, S, W), lambda b, g: (b, 0, base // W + g))

    row_spec = pl.BlockSpec((1, LANES), lambda b, g: (0, 0))
    blk_buf = lambda rows, cols, dt: pltpu.VMEM((DN_HPG, nblk, rows, cols), dt)
    return pl.pallas_call(
        _deltanet_kernel,
        grid=(B, ngrp),
        in_specs=[col_spec(COL_QKV), col_spec(COL_QKV + DN_W), col_spec(COL_QKV + 2 * DN_W), col_spec(COL_Z),
                  pl.BlockSpec((None, S, LANES), lambda b, g: (b, 0, 0), pipeline_mode=pl.Buffered(1)),
                  row_spec, row_spec, row_spec],
        out_specs=pl.BlockSpec((None, S, W), lambda b, g: (b, 0, g)),
        out_shape=jax.ShapeDtypeStruct((B, S, DN_W), BF16),
        scratch_shapes=[pltpu.VMEM((S, LANES), F32), pltpu.VMEM((S, LANES), F32),
                        blk_buf(DN_BLOCK, DN_BLOCK, F32), blk_buf(DN_BLOCK, DN_BLOCK, F32),
                        blk_buf(DN_BLOCK, DN_BLOCK, BF16), blk_buf(DN_BLOCK, DN_BLOCK, BF16),
                        blk_buf(DN_BLOCK, 2 * HEAD_DIM, BF16),
                        blk_buf(2 * DN_BLOCK, HEAD_DIM, BF16),
                        blk_buf(HEAD_DIM, DN_BLOCK, BF16),
                        blk_buf(DN_BLOCK, HEAD_DIM, BF16),
                        blk_buf(8, HEAD_DIM, F32)],
        compiler_params=_cparams(2, DN_VMEM_LIMIT),
        name="deltanet",
    )(P3, P3, P3, P3, G3, alog_row, dtb_row, ng_row)


def _dilated_kernel(q0_ref, q1_ref, q2_ref, k0_ref, k1_ref, k2_ref, v0_ref, v1_ref, v2_ref,
                    pos_ref, invf_ref, o_ref, cos_s, sin_s, qr_s, kr_s, vr_s, og_s, lg_s):
    S = q0_ref.shape[0]
    blk = DIL_BLOCK
    qscale = HEAD_DIM ** -0.5 * LOG2_E

    @pl.when(pl.program_id(1) == 0)
    def _():
        hs, hl = S // 2, HEAD_DIM // 2
        pos = pos_ref[...].astype(F32)
        lane = lax.broadcasted_iota(jnp.int32, (hs, HEAD_DIM), 1)
        low = lane < hl
        ang = jnp.where(low, pos[:hs, :], pos[hs:, :]) * invf_ref[...]
        c, sn = jnp.cos(ang), jnp.sin(ang)
        c_sw, sn_sw = pltpu.roll(c, hl, 1), pltpu.roll(sn, hl, 1)
        cos_s[pl.ds(0, hs), :] = jnp.where(low, c, c_sw)
        cos_s[pl.ds(hs, hs), :] = jnp.where(low, c_sw, c)
        sin_s[pl.ds(0, hs), :] = jnp.where(low, -sn, sn_sw)
        sin_s[pl.ds(hs, hs), :] = jnp.where(low, -sn_sw, sn)

    def rope(x):
        return x * cos_s[...] + pltpu.roll(x, HEAD_DIM // 2, 1) * sin_s[...]

    ri = lax.broadcasted_iota(jnp.int32, (blk, blk), 0)
    ci = lax.broadcasted_iota(jnp.int32, (blk, blk), 1)
    neg = -jnp.inf

    def rows_of(start, dil):
        return pl.ds(start, blk) if dil == 1 else pl.ds(start, blk, stride=dil)

    ones = jnp.ones((blk, HEAD_DIM), BF16)

    def block_group(g, starts, dil, has_prev):
        rows = [rows_of(st, dil) for st in starts]
        q = [qr_s[g, r, :].astype(BF16) for r in rows]
        s = [_dot_nt(q[j], kr_s[g, r, :].astype(BF16)) for j, r in enumerate(rows)]
        if has_prev:
            prows = [rows_of(st - blk * dil, dil) for st in starts]
            sp = [_dot_nt(q[j], kr_s[g, r, :].astype(BF16)) for j, r in enumerate(prows)]
        ps, pps, ms = [], [], []
        for j in range(len(starts)):
            sj = jnp.where(ri >= ci, s[j], neg)
            if has_prev:
                spj = jnp.where(ci >= ri, sp[j], neg)
                m = jnp.max(jnp.maximum(sj, spj), axis=-1, keepdims=True)
                pps.append(jnp.exp2(spj - m).astype(BF16))
            else:
                m = jnp.max(sj, axis=-1, keepdims=True)
            ps.append(jnp.exp2(sj - m).astype(BF16))
            ms.append(m)
        for j, r in enumerate(rows):
            ol = _dot(ps[j], jnp.concatenate([vr_s[g, r, :].astype(BF16), ones], axis=1))
            if has_prev:
                ol = ol + _dot(pps[j], jnp.concatenate([vr_s[g, prows[j], :].astype(BF16), ones], axis=1))
            l = ol[:, HEAD_DIM:]
            og_s[g, r, :] = ol[:, :HEAD_DIM] * (1.0 / l)
            lg_s[g, r, :] = ms[j] * (1.0 / LOG2_E) + jnp.log(l)

    q_refs = (q0_ref, q1_ref, q2_ref)
    k_refs = (k0_ref, k1_ref, k2_ref)
    v_refs = (v0_ref, v1_ref, v2_ref)
    for g in range(len(DIL_GROUPS)):
        qr_s[g] = rope(q_refs[g][...].astype(F32)) * qscale
        kr_s[g] = rope(k_refs[g][...].astype(F32))
        vr_s[g] = v_refs[g][...].astype(F32)
    for g, (window, dil) in enumerate(DIL_GROUPS):
        assert window // dil == blk
        nb = S // dil // blk
        block_group(g, list(range(dil)), dil, False)
        if nb > 1:
            block_group(g, [n * blk * dil + r for n in range(1, nb) for r in range(dil)], dil, True)

    l0, l1, l2 = lg_s[0], lg_s[1], lg_s[2]
    m = jnp.maximum(jnp.maximum(l0, l1), l2)
    e0, e1, e2 = jnp.exp(l0 - m), jnp.exp(l1 - m), jnp.exp(l2 - m)
    inv = 1.0 / (e0 + e1 + e2)
    o_ref[...] = ((e0 * inv) * og_s[0] + (e1 * inv) * og_s[1] + (e2 * inv) * og_s[2]).astype(o_ref.dtype)


def _dilated(P3, pos3, invf_row):
    B, S, _ = P3.shape

    def head_spec(base, g):
        return pl.BlockSpec((None, S, HEAD_DIM), lambda b, h: (b, 0, base // HEAD_DIM + g * DIL_HPG + h))

    ngr = len(DIL_GROUPS)
    in_specs = ([head_spec(COL_BQ, g) for g in range(ngr)] + [head_spec(COL_BK, g) for g in range(ngr)]
                + [head_spec(COL_BV, g) for g in range(ngr)]
                + [pl.BlockSpec((None, S, 1), lambda b, h: (b, 0, 0)),
                   pl.BlockSpec((1, HEAD_DIM), lambda b, h: (0, 0))])
    big = pltpu.VMEM((S, HEAD_DIM), F32)
    return pl.pallas_call(
        _dilated_kernel,
        grid=(B, DIL_HPG),
        in_specs=in_specs,
        out_specs=pl.BlockSpec((None, S, HEAD_DIM), lambda b, h: (b, 0, h)),
        out_shape=jax.ShapeDtypeStruct((B, S, DIL_OUT), BF16),
        scratch_shapes=[big, big] + [pltpu.VMEM((ngr, S, HEAD_DIM), F32)] * 5,
        compiler_params=_cparams(2),
        name="dilated",
    )(*([P3] * 9), pos3, invf_row)


def _mix_kernel(oa_ref, ob_ref, ga_ref, gb_ref, x_ref, wa_ref, wb_ref, wm_ref, g_ref, b_ref, o_ref):
    subs = [slice(r, r + ROW_SUB) for r in range(0, x_ref.shape[0], ROW_SUB)]
    ya = [_dot(oa_ref[r, :], wa_ref[...]) for r in subs]
    yb = [_dot(ob_ref[r, :], wb_ref[...]) for r in subs]
    hs = [(_sigmoid(ga_ref[r, :].astype(F32)) * a + _sigmoid(gb_ref[r, :].astype(F32)) * b).astype(BF16)
          for r, a, b in zip(subs, ya, yb)]
    mix = [_dot(h, wm_ref[...]) for h in hs]
    for r, m in zip(subs, mix):
        o_ref[r, :] = _layer_norm(ALPHA * x_ref[r, :] + m, g_ref[...], b_ref[...])


def _const_spec(shape):
    return pl.BlockSpec(shape, lambda *_: (0,) * len(shape), pipeline_mode=pl.Buffered(1))


def _mix(oa, ob, P, x2d, wa, wb, wm, g, b, tm):
    T = x2d.shape[0]
    row = lambda w, j=0: pl.BlockSpec((tm, w), lambda i: (i, j))
    return pl.pallas_call(
        _mix_kernel,
        grid=(T // tm,),
        in_specs=[row(DN_W), row(DIL_OUT), row(D_MODEL, COL_GA // D_MODEL), row(D_MODEL, COL_GB // D_MODEL),
                  row(D_MODEL), _const_spec(wa.shape), _const_spec(wb.shape), _const_spec(wm.shape),
                  _const_spec(g.shape), _const_spec(b.shape)],
        out_specs=row(D_MODEL),
        out_shape=jax.ShapeDtypeStruct((T, D_MODEL), F32),
        compiler_params=_cparams(1),
        name="mix",
    )(oa, ob, P, P, x2d, wa, wb, wm, g, b)


def _memkv_kernel(m_ref, w_ref, o_ref):
    o_ref[...] = _dot(m_ref[...].astype(BF16), w_ref[...]).astype(o_ref.dtype)


def _mem_kv(mem2d, wkv, tm):
    R = mem2d.shape[0]
    return pl.pallas_call(
        _memkv_kernel,
        grid=(R // tm,),
        in_specs=[pl.BlockSpec((tm, D_MODEL), lambda i: (i, 0)), _const_spec(wkv.shape)],
        out_specs=pl.BlockSpec((tm, 2 * D_MODEL), lambda i: (i, 0)),
        out_shape=jax.ShapeDtypeStruct((R, 2 * D_MODEL), BF16),
        compiler_params=_cparams(1),
        name="mem_kv",
    )(mem2d, wkv)


def _xattn_kernel(x_ref, kv_ref, wq_ref, wo_ref, g_ref, b_ref, o_ref):
    tm = x_ref.shape[0]
    subs = [slice(r, r + ROW_SUB) for r in range(0, tm, ROW_SUB)]
    scale = MEM_HEAD_DIM ** -0.5
    heads = [(slice(h * MEM_HEAD_DIM, (h + 1) * MEM_HEAD_DIM),
              slice(D_MODEL + h * MEM_HEAD_DIM, D_MODEL + (h + 1) * MEM_HEAD_DIM)) for h in range(MEM_HEADS)]
    xs = [x_ref[r, :] for r in subs]
    qs = [_dot(x.astype(BF16), wq_ref[...]) for x in xs]
    ss = [[_dot_nt(q[:, kc].astype(BF16), kv_ref[:, kc]) * scale for kc, _ in heads] for q in qs]
    ps = []
    for s_sub in ss:
        row = []
        for s in s_sub:
            p = jnp.exp(s - jnp.max(s, axis=-1, keepdims=True))
            row.append((p * (1.0 / jnp.sum(p, axis=-1, keepdims=True))).astype(BF16))
        ps.append(row)
    os = [jnp.concatenate([_dot(p, kv_ref[:, vc]).astype(BF16) for p, (_, vc) in zip(p_sub, heads)], axis=1)
          for p_sub in ps]
    cross = [_dot(o, wo_ref[...]) for o in os]
    for r, x, c in zip(subs, xs, cross):
        o_ref[r, :] = _layer_norm(ALPHA * x + c, g_ref[...], b_ref[...])


def _xattn(x3, kv3, wq, wo, g, b, tm):
    B, S, _ = x3.shape
    M = kv3.shape[1]
    return pl.pallas_call(
        _xattn_kernel,
        grid=(B, S // tm),
        in_specs=[pl.BlockSpec((None, tm, D_MODEL), lambda bi, i: (bi, i, 0)),
                  pl.BlockSpec((None, M, 2 * D_MODEL), lambda bi, i: (bi, 0, 0)),
                  _const_spec(wq.shape), _const_spec(wo.shape), _const_spec(g.shape), _const_spec(b.shape)],
        out_specs=pl.BlockSpec((None, tm, D_MODEL), lambda bi, i: (bi, i, 0)),
        out_shape=jax.ShapeDtypeStruct((B, S, D_MODEL), F32),
        compiler_params=_cparams(2),
        name="xattn",
    )(x3, kv3, wq, wo, g, b)


def _ffn_kernel(x_ref, w13_ref, w2_ref, g_ref, b_ref, o_ref):
    x = x_ref[...]
    xb = x.astype(BF16)
    acc = jnp.zeros(x.shape, F32)
    for j in range(D_FF // FF_CHUNK):
        ca = slice(j * FF_CHUNK, (j + 1) * FF_CHUNK)
        cb = slice(D_FF + j * FF_CHUNK, D_FF + (j + 1) * FF_CHUNK)
        a = _dot(xb, w13_ref[:, ca])
        h = (a * _sigmoid(a)) * _dot(xb, w13_ref[:, cb])
        acc = acc + _dot(h.astype(BF16), w2_ref[ca, :])
    o_ref[...] = _layer_norm(ALPHA * x + acc, g_ref[...], b_ref[...])


def _ffn(x2d, w13, w2, g, b, tm):
    T = x2d.shape[0]
    return pl.pallas_call(
        _ffn_kernel,
        grid=(T // tm,),
        in_specs=[pl.BlockSpec((tm, D_MODEL), lambda i: (i, 0)),
                  _const_spec(w13.shape), _const_spec(w2.shape), _const_spec(g.shape), _const_spec(b.shape)],
        out_specs=pl.BlockSpec((tm, D_MODEL), lambda i: (i, 0)),
        out_shape=jax.ShapeDtypeStruct((T, D_MODEL), F32),
        compiler_params=_cparams(1),
        name="ffn",
    )(x2d, w13, w2, g, b)


def _relayout_w_in(w):
    sizes = (3 * DN_W, DN_W, DN_HEADS, DN_HEADS, DIL_W, DIL_W, DIL_W, D_MODEL, D_MODEL)
    offs = [0]
    for sz in sizes:
        offs.append(offs[-1] + sz)
    wt = w.T
    part = lambda i: wt[offs[i]:offs[i + 1], :]
    used = COL_GATES + 2 * DN_HEADS
    pad = jnp.zeros((NP - used, w.shape[0]), w.dtype)
    rows = jnp.concatenate([part(0), part(1), part(7), part(8), part(4), part(5), part(6),
                            part(2), part(3), pad], axis=0)
    return rows.astype(BF16).T


def _pad_row(v, fill=0.0):
    return jnp.concatenate([v.astype(F32), jnp.full((LANES - v.shape[0],), fill, F32)])[None, :]


def kernel(x, mem, positions, w_in, conv_w, a_log, dt_bias, dn_norm_g, w_br_a, w_br_b, w_mix_out,
           ln1_g, ln1_b, w_xq, w_xkv, w_xo, ln2_g, ln2_b, w_ffn13, w_ffn2, ln3_g, ln3_b):
    B, S, D = x.shape
    T = B * S
    M = mem.shape[1]
    assert D == D_MODEL and w_in.shape[0] == 1 and S % (DIL_BLOCK * DIL_GROUPS[-1][1]) == 0
    x2d = x.reshape(T, D)
    bf = lambda w: w[0].astype(BF16)
    row = lambda v: v[0][None, :].astype(F32)

    P, G = _in_proj(x2d, _relayout_w_in(w_in[0]), conv_w[0], tm=S)
    P3 = P.reshape(B, S, NP)

    o_a = _deltanet(P3, G.reshape(B, S, LANES), _pad_row(a_log[0]), _pad_row(dt_bias[0]), row(dn_norm_g))

    inv_freq = ROPE_THETA ** (-jnp.arange(0, HEAD_DIM, 2, dtype=F32) / HEAD_DIM)
    invf_row = jnp.concatenate([inv_freq, inv_freq])[None, :]
    o_b = _dilated(P3, positions.reshape(B, S, 1), invf_row)

    x1 = _mix(o_a.reshape(T, DN_W), o_b.reshape(T, DIL_OUT), P, x2d, bf(w_br_a), bf(w_br_b), bf(w_mix_out),
              row(ln1_g), row(ln1_b), tm=2 * ROW_SUB)

    kv = _mem_kv(mem.reshape(B * M, D), bf(w_xkv), tm=512)
    x2 = _xattn(x1.reshape(B, S, D), kv.reshape(B, M, 2 * D), bf(w_xq), bf(w_xo), row(ln2_g), row(ln2_b), tm=2 * ROW_SUB)

    out = _ffn(x2.reshape(T, D), bf(w_ffn13), bf(w_ffn2), row(ln3_g), row(ln3_b), tm=512)
    return out.reshape(B, S, D)
```

```python
import functools
import math

import jax
import jax.numpy as jnp
from jax import lax
from jax.experimental import pallas as pl
from jax.experimental.pallas import tpu as pltpu

F32 = jnp.float32
BF16 = jnp.bfloat16

LANES = 128
D_MODEL = 1024
DN_HEADS = 8
HEAD_DIM = 128
DN_W = DN_HEADS * HEAD_DIM
DN_CONV = 4
DN_BLOCK = 128
DN_HPG = 4
INV_BASE = 16
DN_PART_BLOCKS = (2, 5, 5, 4)
DIL_GROUPS = ((128, 1), (512, 4), (2048, 16))
DIL_HPG = 4
DIL_W = len(DIL_GROUPS) * DIL_HPG * HEAD_DIM
DIL_OUT = DIL_HPG * HEAD_DIM
DIL_BLOCK = 128
ROPE_THETA = 10000.0
MEM_HEADS = 4
MEM_HEAD_DIM = D_MODEL // MEM_HEADS
D_FF = 2816
FF_CHUNK = 256
ROW_SUB = 512
ALPHA = 2.0 ** 0.25
LN_EPS = 1e-5
RMS_EPS = 1e-6
LOG2_E = 1.4426950408889634

COL_QKV = 0
COL_Z = 3 * DN_W
COL_GA = COL_Z + DN_W
COL_GB = COL_GA + D_MODEL
COL_BQ = COL_GB + D_MODEL
COL_BK = COL_BQ + DIL_W
COL_BV = COL_BK + DIL_W
COL_GATES = COL_BV + DIL_W
PROJ_TN = 1024
PROJ_CONV_ROWS = 512
NP = 11264

VMEM_LIMIT = 52 * 1024 * 1024
DN_VMEM_LIMIT = 58 * 1024 * 1024


def _cparams(n_axes, vmem_limit=VMEM_LIMIT):
    return pltpu.CompilerParams(dimension_semantics=("arbitrary",) * n_axes,
                                vmem_limit_bytes=vmem_limit)


def _dot(a, b):
    return jnp.dot(a, b, preferred_element_type=F32)


def _dot_nt(a, b):
    return lax.dot_general(a, b, (((1,), (1,)), ((), ())), preferred_element_type=F32)


def _sigmoid(x):
    return 1.0 / (1.0 + jnp.exp2(x * -LOG2_E))


def _silu(x):
    return x * _sigmoid(x)


def _layer_norm(y, g, b):
    mu = jnp.mean(y, axis=-1, keepdims=True)
    yc = y - mu
    var = jnp.mean(yc * yc, axis=-1, keepdims=True)
    return yc * lax.rsqrt(var + LN_EPS) * g + b


def _proj_kernel(x_ref, w_ref, cw_ref, o_ref, gates_ref, xb_s):
    S = x_ref.shape[0]
    j = pl.program_id(1)
    n_conv = 3 * DN_W // PROJ_TN

    @pl.when(j == 0)
    def _():
        xb_s[...] = x_ref[...].astype(BF16)

    @pl.when(j < n_conv)
    def _():
        halo = 8
        w = w_ref[...]
        cw = cw_ref[...]
        tail = jnp.zeros((halo, PROJ_TN), F32)
        n_chunks = S // PROJ_CONV_ROWS
        chunk = lambda c: _dot(xb_s[c * PROJ_CONV_ROWS:(c + 1) * PROJ_CONV_ROWS, :], w)
        nxt = chunk(0)
        for c in range(n_chunks):
            rows = slice(c * PROJ_CONV_ROWS, (c + 1) * PROJ_CONV_ROWS)
            acc = nxt
            if c + 1 < n_chunks:
                nxt = chunk(c + 1)
            win = jnp.concatenate([tail, acc], axis=0)
            tail = acc[PROJ_CONV_ROWS - halo:, :]
            y = acc * cw[DN_CONV - 1:DN_CONV, :]
            for t in range(DN_CONV - 1):
                y = y + pltpu.roll(win, DN_CONV - 1 - t, 0)[halo:, :] * cw[t:t + 1, :]
            o_ref[rows, :] = _silu(y).astype(o_ref.dtype)

    @pl.when(j == COL_Z // PROJ_TN)
    def _():
        w = w_ref[...]
        n_chunks = S // PROJ_CONV_ROWS
        chunk = lambda c: _dot(xb_s[c * PROJ_CONV_ROWS:(c + 1) * PROJ_CONV_ROWS, :], w)
        nxt = chunk(0)
        for c in range(n_chunks):
            acc = nxt
            if c + 1 < n_chunks:
                nxt = chunk(c + 1)
            o_ref[c * PROJ_CONV_ROWS:(c + 1) * PROJ_CONV_ROWS, :] = _silu(acc).astype(o_ref.dtype)

    @pl.when(j > COL_Z // PROJ_TN)
    def _():
        acc = _dot(xb_s[...], w_ref[...])
        o_ref[...] = acc.astype(o_ref.dtype)

        @pl.when(j == COL_GATES // PROJ_TN)
        def _():
            lo = COL_GATES % PROJ_TN
            gates_ref[...] = acc[:, lo:lo + LANES]


def _in_proj(x2d, wb, conv_w, tm):
    T = x2d.shape[0]
    n_conv = 3 * DN_W // PROJ_TN
    assert COL_QKV == 0 and COL_Z == 3 * DN_W and DN_W % PROJ_TN == 0
    return pl.pallas_call(
        _proj_kernel,
        grid=(T // tm, NP // PROJ_TN),
        in_specs=[pl.BlockSpec((tm, D_MODEL), lambda i, j: (i, 0)),
                  pl.BlockSpec((D_MODEL, PROJ_TN), lambda i, j: (0, j)),
                  pl.BlockSpec((DN_CONV, PROJ_TN), lambda i, j: (0, jnp.minimum(j, n_conv - 1)))],
        out_specs=[pl.BlockSpec((tm, PROJ_TN), lambda i, j: (i, j)),
                   pl.BlockSpec((tm, LANES), lambda i, j: (i, 0))],
        out_shape=[jax.ShapeDtypeStruct((T, NP), BF16), jax.ShapeDtypeStruct((T, LANES), F32)],
        scratch_shapes=[pltpu.VMEM((tm, D_MODEL), BF16)],
        compiler_params=_cparams(2),
        name="in_proj",
    )(x2d, wb, conv_w)


def _l2norm(x):
    return x * lax.rsqrt(jnp.sum(x * x, axis=-1, keepdims=True) + RMS_EPS)


def _deltanet_kernel(q_ref, k_ref, v_ref, z_ref, gates_ref, alog_ref, dtb_ref, ng_ref, o_ref,
                     gam_s, beta_s, n_s, t_s, p_s, qk_s, rhs_s, mq_s, kdt_s, qd_s, gt_s):
    S = q_ref.shape[0]
    nblk = S // DN_BLOCK
    hg = pl.program_id(1)
    blk = DN_BLOCK
    bc_s, o0_s = n_s, t_s

    @pl.when(hg == 0)
    def _():
        G = gates_ref[...]
        zz = G + dtb_ref[...]
        softplus = jnp.maximum(zz, 0.0) + jnp.log1p(jnp.exp(-jnp.abs(zz)))
        gam = -jnp.exp(alog_ref[...]) * softplus
        rowc = lax.broadcasted_iota(jnp.int32, G.shape, 0) & (blk - 1)
        sh = 1
        while sh < blk:
            gam = gam + jnp.where(rowc >= sh, pltpu.roll(gam, sh, 0), 0.0)
            sh *= 2
        gam_s[...] = gam
        beta_s[...] = _sigmoid(G)

    ri = lax.broadcasted_iota(jnp.int32, (blk, blk), 0)
    ci = lax.broadcasted_iota(jnp.int32, (blk, blk), 1)

    def same_block(b):
        s = int(math.log2(b))
        return (ri >> s) == (ci >> s)

    def setup(c, carry=None):
        rows = pl.ds(pl.multiple_of(c * blk, blk), blk)
        gsl = gam_s[rows, :]
        bsl = beta_s[rows, :]
        for i in range(DN_HPG):
            h = hg * DN_HPG + i
            cols = slice(i * HEAD_DIM, (i + 1) * HEAD_DIM)
            gam_col = jnp.sum(jnp.where(ci == h, gsl, 0.0), axis=1, keepdims=True)
            beta_col = jnp.sum(jnp.where(ci == h + DN_HEADS, bsl, 0.0), axis=1, keepdims=True)
            gb = jnp.broadcast_to(gam_col, (blk, HEAD_DIM))
            kn = _l2norm(k_ref[rows, cols].astype(F32))
            qn = _l2norm(q_ref[rows, cols].astype(F32)) * (HEAD_DIM ** -0.5)
            kb = kn * beta_col
            dec = jnp.exp(jnp.where(ri >= ci, gb - gb.T, -jnp.inf))
            a = _dot_nt(jnp.concatenate([qn, kb], axis=0).astype(BF16), kn.astype(BF16))
            qk_s[i, c] = (a[:blk] * dec).astype(BF16)
            n_s[i, c] = -a[blk:] * jnp.where(ri > ci, dec, 0.0)
            eg = jnp.exp(gb)
            rhs_s[i, c] = jnp.concatenate([v_ref[rows, cols].astype(F32) * beta_col, kb * eg], axis=1).astype(BF16)
            qd_s[i, c] = (qn * eg).astype(BF16)
            gl = jnp.broadcast_to(gb[blk - 1:blk, :], (blk, HEAD_DIM))
            kdt_s[i, c] = (kn * jnp.exp(gl - gb)).T.astype(BF16)
            gt_s[i, c] = jnp.exp(gl[:8, :])
        return carry

    def series_start(i, c):
        d = jnp.where(same_block(INV_BASE), n_s[i, c], 0.0)
        return d, _dot(d.astype(BF16), d.astype(BF16))

    def series_start_done(i, c, r):
        t_s[i, c] = (ri == ci).astype(F32) + r[0]
        p_s[i, c] = r[1].astype(BF16)

    def series_both(i, c):
        pb = p_s[i, c]
        return _dot(jnp.concatenate([t_s[i, c].astype(BF16), pb], axis=0), pb)

    def series_both_done(i, c, r):
        t_s[i, c] = t_s[i, c] + r[:blk]
        p_s[i, c] = r[blk:].astype(BF16)

    def times_p(i, c):
        return _dot(t_s[i, c].astype(BF16), p_s[i, c])

    def add_to_t(i, c, r):
        t_s[i, c] = t_s[i, c] + r

    def to_p(i, c, r):
        p_s[i, c] = r.astype(BF16)

    stages = [(series_start, series_start_done)]
    stages += [(series_both, series_both_done)] * (int(math.log2(INV_BASE)) - 2)
    stages += [(times_p, add_to_t)]
    b = INV_BASE
    while b < blk:
        off = same_block(2 * b) & jnp.logical_not(same_block(b))

        def off_times_t(i, c, off=off):
            return _dot(jnp.where(off, n_s[i, c], 0.0).astype(BF16), t_s[i, c].astype(BF16))

        stages += [(off_times_t, to_p), (times_p, add_to_t)]
        b *= 2

    def solve(i, c):
        return _dot(t_s[i, c].astype(BF16), rhs_s[i, c])

    def solve_done(i, c, r):
        rhs_s[i, c] = r.astype(BF16)

    def cross(i, c):
        return _dot(jnp.concatenate([kdt_s[i, c], qk_s[i, c]], axis=0), rhs_s[i, c])

    def cross_done(i, c, r):
        bc_s[i, c] = r[:blk, :HEAD_DIM]
        o0_s[i, c] = r[blk:, :HEAD_DIM]
        mq_s[i, c] = jnp.concatenate([r[:blk, HEAD_DIM:], qd_s[i, c].astype(F32) - r[blk:, HEAD_DIM:]],
                                     axis=0).astype(BF16)

    stages += [(solve, solve_done), (cross, cross_done)]

    def run_stages(blocks, side_work):
        probs = [(i, c) for c in blocks for i in range(DN_HPG)]
        for k, (issue, finish) in enumerate(stages):
            results = [issue(i, c) for i, c in probs]
            if k < len(side_work):
                side_work[k]()
            for (i, c), r in zip(probs, results):
                finish(i, c, r)

    ng = ng_ref[...]

    def scan_step(c, states):
        rows = pl.ds(pl.multiple_of(c * blk, blk), blk)
        new_states = []
        for i in range(DN_HPG):
            cols = slice(i * HEAD_DIM, (i + 1) * HEAD_DIM)
            st = states[i]
            r = _dot(mq_s[i, c], st.astype(BF16))
            g_tot = jnp.broadcast_to(gt_s[i, c][0:1, :], (HEAD_DIM, HEAD_DIM))
            new_states.append(g_tot * st + bc_s[i, c] - r[:blk])
            o = o0_s[i, c] + r[blk:]
            o = o * lax.rsqrt(jnp.mean(o * o, axis=-1, keepdims=True) + RMS_EPS) * ng
            o_ref[rows, cols] = (o * z_ref[rows, cols].astype(F32)).astype(o_ref.dtype)
        return tuple(new_states)

    assert sum(DN_PART_BLOCKS) == nblk
    starts = [sum(DN_PART_BLOCKS[:p]) for p in range(len(DN_PART_BLOCKS) + 1)]
    parts = [list(range(starts[p], starts[p + 1])) for p in range(len(DN_PART_BLOCKS))]
    lax.fori_loop(0, len(parts[0]), setup, 0)

    state = [tuple(jnp.zeros((HEAD_DIM, HEAD_DIM), F32) for _ in range(DN_HPG))]

    def scan_side(c):
        state[0] = scan_step(c, state[0])

    for p in range(len(parts)):
        side = [functools.partial(setup, c) for c in (parts[p + 1] if p + 1 < len(parts) else [])]
        side += [functools.partial(scan_side, c) for c in (parts[p - 1] if p > 0 else [])]
        assert len(side) <= len(stages)
        run_stages(parts[p], side)
    lax.fori_loop(parts[-1][0], nblk, scan_step, state[0])


def _deltanet(P3, G3, alog_row, dtb_row, ng_row):
    B, S, _ = P3.shape
    W = DN_HPG * HEAD_DIM
    ngrp = DN_HEADS // DN_HPG
    nblk = S // DN_BLOCK

    def col_spec(base):
        return pl.BlockSpec((None, S, W), lambda b, g: (b, 0, base // W + g))

    row_spec = pl.BlockSpec((1, LANES), lambda b, g: (0, 0))
    blk_buf = lambda rows, cols, dt: pltpu.VMEM((DN_HPG, nblk, rows, cols), dt)
    return pl.pallas_call(
        _deltanet_kernel,
        grid=(B, ngrp),
        in_specs=[col_spec(COL_QKV), col_spec(COL_QKV + DN_W), col_spec(COL_QKV + 2 * DN_W), col_spec(COL_Z),
                  pl.BlockSpec((None, S, LANES), lambda b, g: (b, 0, 0), pipeline_mode=pl.Buffered(1)),
                  row_spec, row_spec, row_spec],
        out_specs=pl.BlockSpec((None, S, W), lambda b, g: (b, 0, g)),
        out_shape=jax.ShapeDtypeStruct((B, S, DN_W), BF16),
        scratch_shapes=[pltpu.VMEM((S, LANES), F32), pltpu.VMEM((S, LANES), F32),
                        blk_buf(DN_BLOCK, DN_BLOCK, F32), blk_buf(DN_BLOCK, DN_BLOCK, F32),
                        blk_buf(DN_BLOCK, DN_BLOCK, BF16), blk_buf(DN_BLOCK, DN_BLOCK, BF16),
                        blk_buf(DN_BLOCK, 2 * HEAD_DIM, BF16),
                        blk_buf(2 * DN_BLOCK, HEAD_DIM, BF16),
                        blk_buf(HEAD_DIM, DN_BLOCK, BF16),
                        blk_buf(DN_BLOCK, HEAD_DIM, BF16),
                        blk_buf(8, HEAD_DIM, F32)],
        compiler_params=_cparams(2, DN_VMEM_LIMIT),
        name="deltanet",
    )(P3, P3, P3, P3, G3, alog_row, dtb_row, ng_row)


def _dilated_kernel(q0_ref, q1_ref, q2_ref, k0_ref, k1_ref, k2_ref, v0_ref, v1_ref, v2_ref,
                    pos_ref, invf_ref, o_ref, cos_s, sin_s, qr_s, kr_s, vr_s, og_s, lg_s):
    S = q0_ref.shape[0]
    blk = DIL_BLOCK
    qscale = HEAD_DIM ** -0.5 * LOG2_E

    @pl.when(pl.program_id(1) == 0)
    def _():
        hs, hl = S // 2, HEAD_DIM // 2
        pos = pos_ref[...].astype(F32)
        lane = lax.broadcasted_iota(jnp.int32, (hs, HEAD_DIM), 1)
        low = lane < hl
        ang = jnp.where(low, pos[:hs, :], pos[hs:, :]) * invf_ref[...]
        c, sn = jnp.cos(ang), jnp.sin(ang)
        c_sw, sn_sw = pltpu.roll(c, hl, 1), pltpu.roll(sn, hl, 1)
        cos_s[pl.ds(0, hs), :] = jnp.where(low, c, c_sw)
        cos_s[pl.ds(hs, hs), :] = jnp.where(low, c_sw, c)
        sin_s[pl.ds(0, hs), :] = jnp.where(low, -sn, sn_sw)
        sin_s[pl.ds(hs, hs), :] = jnp.where(low, -sn_sw, sn)

    def rope(x):
        return x * cos_s[...] + pltpu.roll(x, HEAD_DIM // 2, 1) * sin_s[...]

    ri = lax.broadcasted_iota(jnp.int32, (blk, blk), 0)
    ci = lax.broadcasted_iota(jnp.int32, (blk, blk), 1)
    neg = -jnp.inf

    def rows_of(start, dil):
        return pl.ds(start, blk) if dil == 1 else pl.ds(start, blk, stride=dil)

    ones = jnp.ones((blk, HEAD_DIM), BF16)

    def block_group(g, starts, dil, has_prev):
        rows = [rows_of(st, dil) for st in starts]
        q = [qr_s[g, r, :].astype(BF16) for r in rows]
        s = [_dot_nt(q[j], kr_s[g, r, :].astype(BF16)) for j, r in enumerate(rows)]
        if has_prev:
            prows = [rows_of(st - blk * dil, dil) for st in starts]
            sp = [_dot_nt(q[j], kr_s[g, r, :].astype(BF16)) for j, r in enumerate(prows)]
        ps, pps, ms = [], [], []
        for j in range(len(starts)):
            sj = jnp.where(ri >= ci, s[j], neg)
            if has_prev:
                spj = jnp.where(ci >= ri, sp[j], neg)
                m = jnp.max(jnp.maximum(sj, spj), axis=-1, keepdims=True)
                pps.append(jnp.exp2(spj - m).astype(BF16))
            else:
                m = jnp.max(sj, axis=-1, keepdims=True)
            ps.append(jnp.exp2(sj - m).astype(BF16))
            ms.append(m)
        for j, r in enumerate(rows):
            ol = _dot(ps[j], jnp.concatenate([vr_s[g, r, :].astype(BF16), ones], axis=1))
            if has_prev:
                ol = ol + _dot(pps[j], jnp.concatenate([vr_s[g, prows[j], :].astype(BF16), ones], axis=1))
            l = ol[:, HEAD_DIM:]
            og_s[g, r, :] = ol[:, :HEAD_DIM] * (1.0 / l)
            lg_s[g, r, :] = ms[j] * (1.0 / LOG2_E) + jnp.log(l)

    q_refs = (q0_ref, q1_ref, q2_ref)
    k_refs = (k0_ref, k1_ref, k2_ref)
    v_refs = (v0_ref, v1_ref, v2_ref)
    for g in range(len(DIL_GROUPS)):
        qr_s[g] = rope(q_refs[g][...].astype(F32)) * qscale
        kr_s[g] = rope(k_refs[g][...].astype(F32))
        vr_s[g] = v_refs[g][...].astype(F32)
    for g, (window, dil) in enumerate(DIL_GROUPS):
        assert window // dil == blk
        nb = S // dil // blk
        block_group(g, list(range(dil)), dil, False)
        if nb > 1:
            block_group(g, [n * blk * dil + r for n in range(1, nb) for r in range(dil)], dil, True)

    l0, l1, l2 = lg_s[0], lg_s[1], lg_s[2]
    m = jnp.maximum(jnp.maximum(l0, l1), l2)
    e0, e1, e2 = jnp.exp(l0 - m), jnp.exp(l1 - m), jnp.exp(l2 - m)
    inv = 1.0 / (e0 + e1 + e2)
    o_ref[...] = ((e0 * inv) * og_s[0] + (e1 * inv) * og_s[1] + (e2 * inv) * og_s[2]).astype(o_ref.dtype)


def _dilated(P3, pos3, invf_row):
    B, S, _ = P3.shape

    def head_spec(base, g):
        return pl.BlockSpec((None, S, HEAD_DIM), lambda b, h: (b, 0, base // HEAD_DIM + g * DIL_HPG + h))

    ngr = len(DIL_GROUPS)
    in_specs = ([head_spec(COL_BQ, g) for g in range(ngr)] + [head_spec(COL_BK, g) for g in range(ngr)]
                + [head_spec(COL_BV, g) for g in range(ngr)]
                + [pl.BlockSpec((None, S, 1), lambda b, h: (b, 0, 0)),
                   pl.BlockSpec((1, HEAD_DIM), lambda b, h: (0, 0))])
    big = pltpu.VMEM((S, HEAD_DIM), F32)
    return pl.pallas_call(
        _dilated_kernel,
        grid=(B, DIL_HPG),
        in_specs=in_specs,
        out_specs=pl.BlockSpec((None, S, HEAD_DIM), lambda b, h: (b, 0, h)),
        out_shape=jax.ShapeDtypeStruct((B, S, DIL_OUT), BF16),
        scratch_shapes=[big, big] + [pltpu.VMEM((ngr, S, HEAD_DIM), F32)] * 5,
        compiler_params=_cparams(2),
        name="dilated",
    )(*([P3] * 9), pos3, invf_row)


def _mix_kernel(oa_ref, ob_ref, ga_ref, gb_ref, x_ref, wa_ref, wb_ref, wm_ref, g_ref, b_ref, o_ref):
    subs = [slice(r, r + ROW_SUB) for r in range(0, x_ref.shape[0], ROW_SUB)]
    ya = [_dot(oa_ref[r, :], wa_ref[...]) for r in subs]
    yb = [_dot(ob_ref[r, :], wb_ref[...]) for r in subs]
    hs = [(_sigmoid(ga_ref[r, :].astype(F32)) * a + _sigmoid(gb_ref[r, :].astype(F32)) * b).astype(BF16)
          for r, a, b in zip(subs, ya, yb)]
    mix = [_dot(h, wm_ref[...]) for h in hs]
    for r, m in zip(subs, mix):
        o_ref[r, :] = _layer_norm(ALPHA * x_ref[r, :] + m, g_ref[...], b_ref[...])


def _const_spec(shape):
    return pl.BlockSpec(shape, lambda *_: (0,) * len(shape), pipeline_mode=pl.Buffered(1))


def _mix(oa, ob, P, x2d, wa, wb, wm, g, b, tm):
    T = x2d.shape[0]
    row = lambda w, j=0: pl.BlockSpec((tm, w), lambda i: (i, j))
    return pl.pallas_call(
        _mix_kernel,
        grid=(T // tm,),
        in_specs=[row(DN_W), row(DIL_OUT), row(D_MODEL, COL_GA // D_MODEL), row(D_MODEL, COL_GB // D_MODEL),
                  row(D_MODEL), _const_spec(wa.shape), _const_spec(wb.shape), _const_spec(wm.shape),
                  _const_spec(g.shape), _const_spec(b.shape)],
        out_specs=row(D_MODEL),
        out_shape=jax.ShapeDtypeStruct((T, D_MODEL), F32),
        compiler_params=_cparams(1),
        name="mix",
    )(oa, ob, P, P, x2d, wa, wb, wm, g, b)


def _memkv_kernel(m_ref, w_ref, o_ref):
    o_ref[...] = _dot(m_ref[...].astype(BF16), w_ref[...]).astype(o_ref.dtype)


def _mem_kv(mem2d, wkv, tm):
    R = mem2d.shape[0]
    return pl.pallas_call(
        _memkv_kernel,
        grid=(R // tm,),
        in_specs=[pl.BlockSpec((tm, D_MODEL), lambda i: (i, 0)), _const_spec(wkv.shape)],
        out_specs=pl.BlockSpec((tm, 2 * D_MODEL), lambda i: (i, 0)),
        out_shape=jax.ShapeDtypeStruct((R, 2 * D_MODEL), BF16),
        compiler_params=_cparams(1),
        name="mem_kv",
    )(mem2d, wkv)


def _xattn_kernel(x_ref, kv_ref, wq_ref, wo_ref, g_ref, b_ref, o_ref):
    tm = x_ref.shape[0]
    subs = [slice(r, r + ROW_SUB) for r in range(0, tm, ROW_SUB)]
    scale = MEM_HEAD_DIM ** -0.5
    heads = [(slice(h * MEM_HEAD_DIM, (h + 1) * MEM_HEAD_DIM),
              slice(D_MODEL + h * MEM_HEAD_DIM, D_MODEL + (h + 1) * MEM_HEAD_DIM)) for h in range(MEM_HEADS)]
    xs = [x_ref[r, :] for r in subs]
    qs = [_dot(x.astype(BF16), wq_ref[...]) for x in xs]
    ss = [[_dot_nt(q[:, kc].astype(BF16), kv_ref[:, kc]) * scale for kc, _ in heads] for q in qs]
    ps = []
    for s_sub in ss:
        row = []
        for s in s_sub:
            p = jnp.exp(s - jnp.max(s, axis=-1, keepdims=True))
            row.append((p * (1.0 / jnp.sum(p, axis=-1, keepdims=True))).astype(BF16))
        ps.append(row)
    os = [jnp.concatenate([_dot(p, kv_ref[:, vc]).astype(BF16) for p, (_, vc) in zip(p_sub, heads)], axis=1)
          for p_sub in ps]
    cross = [_dot(o, wo_ref[...]) for o in os]
    for r, x, c in zip(subs, xs, cross):
        o_ref[r, :] = _layer_norm(ALPHA * x + c, g_ref[...], b_ref[...])


def _xattn(x3, kv3, wq, wo, g, b, tm):
    B, S, _ = x3.shape
    M = kv3.shape[1]
    return pl.pallas_call(
        _xattn_kernel,
        grid=(B, S // tm),
        in_specs=[pl.BlockSpec((None, tm, D_MODEL), lambda bi, i: (bi, i, 0)),
                  pl.BlockSpec((None, M, 2 * D_MODEL), lambda bi, i: (bi, 0, 0)),
                  _const_spec(wq.shape), _const_spec(wo.shape), _const_spec(g.shape), _const_spec(b.shape)],
        out_specs=pl.BlockSpec((None, tm, D_MODEL), lambda bi, i: (bi, i, 0)),
        out_shape=jax.ShapeDtypeStruct((B, S, D_MODEL), F32),
        compiler_params=_cparams(2),
        name="xattn",
    )(x3, kv3, wq, wo, g, b)


def _ffn_kernel(x_ref, w13_ref, w2_ref, g_ref, b_ref, o_ref):
    x = x_ref[...]
    xb = x.astype(BF16)
    acc = jnp.zeros(x.shape, F32)
    for j in range(D_FF // FF_CHUNK):
        ca = slice(j * FF_CHUNK, (j + 1) * FF_CHUNK)
        cb = slice(D_FF + j * FF_CHUNK, D_FF + (j + 1) * FF_CHUNK)
        a = _dot(xb, w13_ref[:, ca])
        h = (a * _sigmoid(a)) * _dot(xb, w13_ref[:, cb])
        acc = acc + _dot(h.astype(BF16), w2_ref[ca, :])
    o_ref[...] = _layer_norm(ALPHA * x + acc, g_ref[...], b_ref[...])


def _ffn(x2d, w13, w2, g, b, tm):
    T = x2d.shape[0]
    return pl.pallas_call(
        _ffn_kernel,
        grid=(T // tm,),
        in_specs=[pl.BlockSpec((tm, D_MODEL), lambda i: (i, 0)),
                  _const_spec(w13.shape), _const_spec(w2.shape), _const_spec(g.shape), _const_spec(b.shape)],
        out_specs=pl.BlockSpec((tm, D_MODEL), lambda i: (i, 0)),
        out_shape=jax.ShapeDtypeStruct((T, D_MODEL), F32),
        compiler_params=_cparams(1),
        name="ffn",
    )(x2d, w13, w2, g, b)


def _relayout_w_in(w):
    sizes = (3 * DN_W, DN_W, DN_HEADS, DN_HEADS, DIL_W, DIL_W, DIL_W, D_MODEL, D_MODEL)
    offs = [0]
    for sz in sizes:
        offs.append(offs[-1] + sz)
    wt = w.T
    part = lambda i: wt[offs[i]:offs[i + 1], :]
    used = COL_GATES + 2 * DN_HEADS
    pad = jnp.zeros((NP - used, w.shape[0]), w.dtype)
    rows = jnp.concatenate([part(0), part(1), part(7), part(8), part(4), part(5), part(6),
                            part(2), part(3), pad], axis=0)
    return rows.astype(BF16).T


def _pad_row(v, fill=0.0):
    return jnp.concatenate([v.astype(F32), jnp.full((LANES - v.shape[0],), fill, F32)])[None, :]


def kernel(x, mem, positions, w_in, conv_w, a_log, dt_bias, dn_norm_g, w_br_a, w_br_b, w_mix_out,
           ln1_g, ln1_b, w_xq, w_xkv, w_xo, ln2_g, ln2_b, w_ffn13, w_ffn2, ln3_g, ln3_b):
    B, S, D = x.shape
    T = B * S
    M = mem.shape[1]
    assert D == D_MODEL and w_in.shape[0] == 1 and S % (DIL_BLOCK * DIL_GROUPS[-1][1]) == 0
    x2d = x.reshape(T, D)
    bf = lambda w: w[0].astype(BF16)
    row = lambda v: v[0][None, :].astype(F32)

    P, G = _in_proj(x2d, _relayout_w_in(w_in[0]), conv_w[0], tm=S)
    P3 = P.reshape(B, S, NP)

    o_a = _deltanet(P3, G.reshape(B, S, LANES), _pad_row(a_log[0]), _pad_row(dt_bias[0]), row(dn_norm_g))

    inv_freq = ROPE_THETA ** (-jnp.arange(0, HEAD_DIM, 2, dtype=F32) / HEAD_DIM)
    invf_row = jnp.concatenate([inv_freq, inv_freq])[None, :]
    o_b = _dilated(P3, positions.reshape(B, S, 1), invf_row)

    x1 = _mix(o_a.reshape(T, DN_W), o_b.reshape(T, DIL_OUT), P, x2d, bf(w_br_a), bf(w_br_b), bf(w_mix_out),
              row(ln1_g), row(ln1_b), tm=2 * ROW_SUB)

    kv = _mem_kv(mem.reshape(B * M, D), bf(w_xkv), tm=512)
    x2 = _xattn(x1.reshape(B, S, D), kv.reshape(B, M, 2 * D), bf(w_xq), bf(w_xo), row(ln2_g), row(ln2_b), tm=2 * ROW_SUB)

    out = _ffn(x2.reshape(T, D), bf(w_ffn13), bf(w_ffn2), row(ln3_g), row(ln3_b), tm=512)
    return out.reshape(B, S, D)
```

```python
import functools
import math

import jax
import jax.numpy as jnp
from jax import lax
from jax.experimental import pallas as pl
from jax.experimental.pallas import tpu as pltpu

F32 = jnp.float32
BF16 = jnp.bfloat16

LANES = 128
D_MODEL = 1024
DN_HEADS = 8
HEAD_DIM = 128
DN_W = DN_HEADS * HEAD_DIM
DN_CONV = 4
DN_BLOCK = 128
DN_HPG = 4
INV_BASE = 16
DN_PART_BLOCKS = (2, 5, 5, 4)
DIL_GROUPS = ((128, 1), (512, 4), (2048, 16))
DIL_HPG = 4
DIL_W = len(DIL_GROUPS) * DIL_HPG * HEAD_DIM
DIL_OUT = DIL_HPG * HEAD_DIM
DIL_BLOCK = 128
ROPE_THETA = 10000.0
MEM_HEADS = 4
MEM_HEAD_DIM = D_MODEL // MEM_HEADS
D_FF = 2816
FF_CHUNK = 256
ROW_SUB = 512
ALPHA = 2.0 ** 0.25
LN_EPS = 1e-5
RMS_EPS = 1e-6
LOG2_E = 1.4426950408889634

COL_QKV = 0
COL_Z = 3 * DN_W
COL_GA = COL_Z + DN_W
COL_GB = COL_GA + D_MODEL
COL_BQ = COL_GB + D_MODEL
COL_BK = COL_BQ + DIL_W
COL_BV = COL_BK + DIL_W
COL_GATES = COL_BV + DIL_W
PROJ_TN = 1024
PROJ_CONV_ROWS = 1024
NP = 11264

VMEM_LIMIT = 52 * 1024 * 1024
DN_VMEM_LIMIT = 58 * 1024 * 1024


def _cparams(n_axes, vmem_limit=VMEM_LIMIT):
    return pltpu.CompilerParams(dimension_semantics=("arbitrary",) * n_axes,
                                vmem_limit_bytes=vmem_limit)


def _dot(a, b):
    return jnp.dot(a, b, preferred_element_type=F32)


def _dot_nt(a, b):
    return lax.dot_general(a, b, (((1,), (1,)), ((), ())), preferred_element_type=F32)


def _sigmoid(x):
    return 1.0 / (1.0 + jnp.exp2(x * -LOG2_E))


def _silu(x):
    return x * _sigmoid(x)


def _layer_norm(y, g, b):
    mu = jnp.mean(y, axis=-1, keepdims=True)
    yc = y - mu
    var = jnp.mean(yc * yc, axis=-1, keepdims=True)
    return yc * lax.rsqrt(var + LN_EPS) * g + b


def _proj_kernel(x_ref, w_ref, cw_ref, o_ref, gates_ref, xb_s):
    S = x_ref.shape[0]
    j = pl.program_id(1)
    n_conv = 3 * DN_W // PROJ_TN

    @pl.when(j == 0)
    def _():
        xb_s[...] = x_ref[...].astype(BF16)

    @pl.when(j < n_conv)
    def _():
        halo = 8
        w = w_ref[...]
        cw = cw_ref[...]
        tail = jnp.zeros((halo, PROJ_TN), F32)
        n_chunks = S // PROJ_CONV_ROWS
        chunk = lambda c: _dot(xb_s[c * PROJ_CONV_ROWS:(c + 1) * PROJ_CONV_ROWS, :], w)
        nxt = chunk(0)
        for c in range(n_chunks):
            rows = slice(c * PROJ_CONV_ROWS, (c + 1) * PROJ_CONV_ROWS)
            acc = nxt
            if c + 1 < n_chunks:
                nxt = chunk(c + 1)
            win = jnp.concatenate([tail, acc], axis=0)
            tail = acc[PROJ_CONV_ROWS - halo:, :]
            y = acc * cw[DN_CONV - 1:DN_CONV, :]
            for t in range(DN_CONV - 1):
                y = y + pltpu.roll(win, DN_CONV - 1 - t, 0)[halo:, :] * cw[t:t + 1, :]
            o_ref[rows, :] = _silu(y).astype(o_ref.dtype)

    @pl.when(j == COL_Z // PROJ_TN)
    def _():
        w = w_ref[...]
        n_chunks = S // PROJ_CONV_ROWS
        chunk = lambda c: _dot(xb_s[c * PROJ_CONV_ROWS:(c + 1) * PROJ_CONV_ROWS, :], w)
        nxt = chunk(0)
        for c in range(n_chunks):
            acc = nxt
            if c + 1 < n_chunks:
                nxt = chunk(c + 1)
            o_ref[c * PROJ_CONV_ROWS:(c + 1) * PROJ_CONV_ROWS, :] = _silu(acc).astype(o_ref.dtype)

    @pl.when(j > COL_Z // PROJ_TN)
    def _():
        acc = _dot(xb_s[...], w_ref[...])
        o_ref[...] = acc.astype(o_ref.dtype)

        @pl.when(j == COL_GATES // PROJ_TN)
        def _():
            lo = COL_GATES % PROJ_TN
            gates_ref[...] = acc[:, lo:lo + LANES]


def _in_proj(x2d, wb, conv_w, tm):
    T = x2d.shape[0]
    n_conv = 3 * DN_W // PROJ_TN
    assert COL_QKV == 0 and COL_Z == 3 * DN_W and DN_W % PROJ_TN == 0
    return pl.pallas_call(
        _proj_kernel,
        grid=(T // tm, NP // PROJ_TN),
        in_specs=[pl.BlockSpec((tm, D_MODEL), lambda i, j: (i, 0)),
                  pl.BlockSpec((D_MODEL, PROJ_TN), lambda i, j: (0, j)),
                  pl.BlockSpec((DN_CONV, PROJ_TN), lambda i, j: (0, jnp.minimum(j, n_conv - 1)))],
        out_specs=[pl.BlockSpec((tm, PROJ_TN), lambda i, j: (i, j)),
                   pl.BlockSpec((tm, LANES), lambda i, j: (i, 0))],
        out_shape=[jax.ShapeDtypeStruct((T, NP), BF16), jax.ShapeDtypeStruct((T, LANES), F32)],
        scratch_shapes=[pltpu.VMEM((tm, D_MODEL), BF16)],
        compiler_params=_cparams(2),
        name="in_proj",
    )(x2d, wb, conv_w)


def _l2norm(x):
    return x * lax.rsqrt(jnp.sum(x * x, axis=-1, keepdims=True) + RMS_EPS)


def _deltanet_kernel(q_ref, k_ref, v_ref, z_ref, gates_ref, alog_ref, dtb_ref, ng_ref, o_ref,
                     gam_s, beta_s, n_s, t_s, p_s, qk_s, rhs_s, mq_s, kdt_s, qd_s, gt_s):
    S = q_ref.shape[0]
    nblk = S // DN_BLOCK
    hg = pl.program_id(1)
    blk = DN_BLOCK
    bc_s, o0_s = n_s, t_s

    @pl.when(hg == 0)
    def _():
        G = gates_ref[...]
        zz = G + dtb_ref[...]
        softplus = jnp.maximum(zz, 0.0) + jnp.log1p(jnp.exp(-jnp.abs(zz)))
        gam = -jnp.exp(alog_ref[...]) * softplus
        rowc = lax.broadcasted_iota(jnp.int32, G.shape, 0) & (blk - 1)
        sh = 1
        while sh < blk:
            gam = gam + jnp.where(rowc >= sh, pltpu.roll(gam, sh, 0), 0.0)
            sh *= 2
        gam_s[...] = gam
        beta_s[...] = _sigmoid(G)

    ri = lax.broadcasted_iota(jnp.int32, (blk, blk), 0)
    ci = lax.broadcasted_iota(jnp.int32, (blk, blk), 1)

    def same_block(b):
        s = int(math.log2(b))
        return (ri >> s) == (ci >> s)

    def setup(c, carry=None):
        rows = pl.ds(pl.multiple_of(c * blk, blk), blk)
        gsl = gam_s[rows, :]
        bsl = beta_s[rows, :]
        for i in range(DN_HPG):
            h = hg * DN_HPG + i
            cols = slice(i * HEAD_DIM, (i + 1) * HEAD_DIM)
            gam_col = jnp.sum(jnp.where(ci == h, gsl, 0.0), axis=1, keepdims=True)
            beta_col = jnp.sum(jnp.where(ci == h + DN_HEADS, bsl, 0.0), axis=1, keepdims=True)
            gb = jnp.broadcast_to(gam_col, (blk, HEAD_DIM))
            kn = _l2norm(k_ref[rows, cols].astype(F32))
            qn = _l2norm(q_ref[rows, cols].astype(F32)) * (HEAD_DIM ** -0.5)
            kb = kn * beta_col
            dec = jnp.exp(jnp.where(ri >= ci, gb - gb.T, -jnp.inf))
            a = _dot_nt(jnp.concatenate([qn, kb], axis=0).astype(BF16), kn.astype(BF16))
            qk_s[i, c] = (a[:blk] * dec).astype(BF16)
            n_s[i, c] = -a[blk:] * jnp.where(ri > ci, dec, 0.0)
            eg = jnp.exp(gb)
            rhs_s[i, c] = jnp.concatenate([v_ref[rows, cols].astype(F32) * beta_col, kb * eg], axis=1).astype(BF16)
            qd_s[i, c] = (qn * eg).astype(BF16)
            gl = jnp.broadcast_to(gb[blk - 1:blk, :], (blk, HEAD_DIM))
            kdt_s[i, c] = (kn * jnp.exp(gl - gb)).T.astype(BF16)
            gt_s[i, c] = jnp.exp(gl[:8, :])
        return carry

    def series_start(i, c):
        d = jnp.where(same_block(INV_BASE), n_s[i, c], 0.0)
        return d, _dot(d.astype(BF16), d.astype(BF16))

    def series_start_done(i, c, r):
        t_s[i, c] = (ri == ci).astype(F32) + r[0]
        p_s[i, c] = r[1].astype(BF16)

    def series_both(i, c):
        pb = p_s[i, c]
        return _dot(jnp.concatenate([t_s[i, c].astype(BF16), pb], axis=0), pb)

    def series_both_done(i, c, r):
        t_s[i, c] = t_s[i, c] + r[:blk]
        p_s[i, c] = r[blk:].astype(BF16)

    def times_p(i, c):
        return _dot(t_s[i, c].astype(BF16), p_s[i, c])

    def add_to_t(i, c, r):
        t_s[i, c] = t_s[i, c] + r

    def to_p(i, c, r):
        p_s[i, c] = r.astype(BF16)

    stages = [(series_start, series_start_done)]
    stages += [(series_both, series_both_done)] * (int(math.log2(INV_BASE)) - 2)
    stages += [(times_p, add_to_t)]
    b = INV_BASE
    while b < blk:
        off = same_block(2 * b) & jnp.logical_not(same_block(b))

        def off_times_t(i, c, off=off):
            return _dot(jnp.where(off, n_s[i, c], 0.0).astype(BF16), t_s[i, c].astype(BF16))

        stages += [(off_times_t, to_p), (times_p, add_to_t)]
        b *= 2

    def solve(i, c):
        return _dot(t_s[i, c].astype(BF16), rhs_s[i, c])

    def solve_done(i, c, r):
        rhs_s[i, c] = r.astype(BF16)

    def cross(i, c):
        return _dot(jnp.concatenate([kdt_s[i, c], qk_s[i, c]], axis=0), rhs_s[i, c])

    def cross_done(i, c, r):
        bc_s[i, c] = r[:blk, :HEAD_DIM]
        o0_s[i, c] = r[blk:, :HEAD_DIM]
        mq_s[i, c] = jnp.concatenate([r[:blk, HEAD_DIM:], qd_s[i, c].astype(F32) - r[blk:, HEAD_DIM:]],
                                     axis=0).astype(BF16)

    stages += [(solve, solve_done), (cross, cross_done)]

    def run_stages(blocks, side_work):
        probs = [(i, c) for c in blocks for i in range(DN_HPG)]
        for k, (issue, finish) in enumerate(stages):
            results = [issue(i, c) for i, c in probs]
            if k < len(side_work):
                side_work[k]()
            for (i, c), r in zip(probs, results):
                finish(i, c, r)

    ng = ng_ref[...]

    def scan_step(c, states):
        rows = pl.ds(pl.multiple_of(c * blk, blk), blk)
        new_states = []
        for i in range(DN_HPG):
            cols = slice(i * HEAD_DIM, (i + 1) * HEAD_DIM)
            st = states[i]
            r = _dot(mq_s[i, c], st.astype(BF16))
            g_tot = jnp.broadcast_to(gt_s[i, c][0:1, :], (HEAD_DIM, HEAD_DIM))
            new_states.append(g_tot * st + bc_s[i, c] - r[:blk])
            o = o0_s[i, c] + r[blk:]
            o = o * lax.rsqrt(jnp.mean(o * o, axis=-1, keepdims=True) + RMS_EPS) * ng
            o_ref[rows, cols] = (o * z_ref[rows, cols].astype(F32)).astype(o_ref.dtype)
        return tuple(new_states)

    assert sum(DN_PART_BLOCKS) == nblk
    starts = [sum(DN_PART_BLOCKS[:p]) for p in range(len(DN_PART_BLOCKS) + 1)]
    parts = [list(range(starts[p], starts[p + 1])) for p in range(len(DN_PART_BLOCKS))]
    lax.fori_loop(0, len(parts[0]), setup, 0)

    state = [tuple(jnp.zeros((HEAD_DIM, HEAD_DIM), F32) for _ in range(DN_HPG))]

    def scan_side(c):
        state[0] = scan_step(c, state[0])

    for p in range(len(parts)):
        side = [functools.partial(setup, c) for c in (parts[p + 1] if p + 1 < len(parts) else [])]
        side += [functools.partial(scan_side, c) for c in (parts[p - 1] if p > 0 else [])]
        assert len(side) <= len(stages)
        run_stages(parts[p], side)
    lax.fori_loop(parts[-1][0], nblk, scan_step, state[0])


def _deltanet(P3, G3, alog_row, dtb_row, ng_row):
    B, S, _ = P3.shape
    W = DN_HPG * HEAD_DIM
    ngrp = DN_HEADS // DN_HPG
    nblk = S // DN_BLOCK

    def col_spec(base):
        return pl.BlockSpec((None, S, W), lambda b, g: (b, 0, base // W + g))

    row_spec = pl.BlockSpec((1, LANES), lambda b, g: (0, 0))
    blk_buf = lambda rows, cols, dt: pltpu.VMEM((DN_HPG, nblk, rows, cols), dt)
    return pl.pallas_call(
        _deltanet_kernel,
        grid=(B, ngrp),
        in_specs=[col_spec(COL_QKV), col_spec(COL_QKV + DN_W), col_spec(COL_QKV + 2 * DN_W), col_spec(COL_Z),
                  pl.BlockSpec((None, S, LANES), lambda b, g: (b, 0, 0), pipeline_mode=pl.Buffered(1)),
                  row_spec, row_spec, row_spec],
        out_specs=pl.BlockSpec((None, S, W), lambda b, g: (b, 0, g)),
        out_shape=jax.ShapeDtypeStruct((B, S, DN_W), BF16),
        scratch_shapes=[pltpu.VMEM((S, LANES), F32), pltpu.VMEM((S, LANES), F32),
                        blk_buf(DN_BLOCK, DN_BLOCK, F32), blk_buf(DN_BLOCK, DN_BLOCK, F32),
                        blk_buf(DN_BLOCK, DN_BLOCK, BF16), blk_buf(DN_BLOCK, DN_BLOCK, BF16),
                        blk_buf(DN_BLOCK, 2 * HEAD_DIM, BF16),
                        blk_buf(2 * DN_BLOCK, HEAD_DIM, BF16),
                        blk_buf(HEAD_DIM, DN_BLOCK, BF16),
                        blk_buf(DN_BLOCK, HEAD_DIM, BF16),
                        blk_buf(8, HEAD_DIM, F32)],
        compiler_params=_cparams(2, DN_VMEM_LIMIT),
        name="deltanet",
    )(P3, P3, P3, P3, G3, alog_row, dtb_row, ng_row)


def _dilated_kernel(q0_ref, q1_ref, q2_ref, k0_ref, k1_ref, k2_ref, v0_ref, v1_ref, v2_ref,
                    pos_ref, invf_ref, o_ref, cos_s, sin_s, qr_s, kr_s, vr_s, og_s, lg_s):
    S = q0_ref.shape[0]
    blk = DIL_BLOCK
    qscale = HEAD_DIM ** -0.5 * LOG2_E

    @pl.when(pl.program_id(1) == 0)
    def _():
        hs, hl = S // 2, HEAD_DIM // 2
        pos = pos_ref[...].astype(F32)
        lane = lax.broadcasted_iota(jnp.int32, (hs, HEAD_DIM), 1)
        low = lane < hl
        ang = jnp.where(low, pos[:hs, :], pos[hs:, :]) * invf_ref[...]
        c, sn = jnp.cos(ang), jnp.sin(ang)
        c_sw, sn_sw = pltpu.roll(c, hl, 1), pltpu.roll(sn, hl, 1)
        cos_s[pl.ds(0, hs), :] = jnp.where(low, c, c_sw)
        cos_s[pl.ds(hs, hs), :] = jnp.where(low, c_sw, c)
        sin_s[pl.ds(0, hs), :] = jnp.where(low, -sn, sn_sw)
        sin_s[pl.ds(hs, hs), :] = jnp.where(low, -sn_sw, sn)

    def rope(x):
        return x * cos_s[...] + pltpu.roll(x, HEAD_DIM // 2, 1) * sin_s[...]

    ri = lax.broadcasted_iota(jnp.int32, (blk, blk), 0)
    ci = lax.broadcasted_iota(jnp.int32, (blk, blk), 1)
    neg = -jnp.inf

    def rows_of(start, dil):
        return pl.ds(start, blk) if dil == 1 else pl.ds(start, blk, stride=dil)

    ones = jnp.ones((blk, HEAD_DIM), BF16)

    def block_group(g, starts, dil, has_prev):
        rows = [rows_of(st, dil) for st in starts]
        q = [qr_s[g, r, :].astype(BF16) for r in rows]
        s = [_dot_nt(q[j], kr_s[g, r, :].astype(BF16)) for j, r in enumerate(rows)]
        if has_prev:
            prows = [rows_of(st - blk * dil, dil) for st in starts]
            sp = [_dot_nt(q[j], kr_s[g, r, :].astype(BF16)) for j, r in enumerate(prows)]
        ps, pps, ms = [], [], []
        for j in range(len(starts)):
            sj = jnp.where(ri >= ci, s[j], neg)
            if has_prev:
                spj = jnp.where(ci >= ri, sp[j], neg)
                m = jnp.max(jnp.maximum(sj, spj), axis=-1, keepdims=True)
                pps.append(jnp.exp2(spj - m).astype(BF16))
            else:
                m = jnp.max(sj, axis=-1, keepdims=True)
            ps.append(jnp.exp2(sj - m).astype(BF16))
            ms.append(m)
        for j, r in enumerate(rows):
            ol = _dot(ps[j], jnp.concatenate([vr_s[g, r, :].astype(BF16), ones], axis=1))
            if has_prev:
                ol = ol + _dot(pps[j], jnp.concatenate([vr_s[g, prows[j], :].astype(BF16), ones], axis=1))
            l = ol[:, HEAD_DIM:]
            og_s[g, r, :] = ol[:, :HEAD_DIM] * (1.0 / l)
            lg_s[g, r, :] = ms[j] * (1.0 / LOG2_E) + jnp.log(l)

    q_refs = (q0_ref, q1_ref, q2_ref)
    k_refs = (k0_ref, k1_ref, k2_ref)
    v_refs = (v0_ref, v1_ref, v2_ref)
    for g in range(len(DIL_GROUPS)):
        qr_s[g] = rope(q_refs[g][...].astype(F32)) * qscale
        kr_s[g] = rope(k_refs[g][...].astype(F32))
        vr_s[g] = v_refs[g][...].astype(F32)
    for g, (window, dil) in enumerate(DIL_GROUPS):
        assert window // dil == blk
        nb = S // dil // blk
        block_group(g, list(range(dil)), dil, False)
        if nb > 1:
            block_group(g, [n * blk * dil + r for n in range(1, nb) for r in range(dil)], dil, True)

    l0, l1, l2 = lg_s[0], lg_s[1], lg_s[2]
    m = jnp.maximum(jnp.maximum(l0, l1), l2)
    e0, e1, e2 = jnp.exp(l0 - m), jnp.exp(l1 - m), jnp.exp(l2 - m)
    inv = 1.0 / (e0 + e1 + e2)
    o_ref[...] = ((e0 * inv) * og_s[0] + (e1 * inv) * og_s[1] + (e2 * inv) * og_s[2]).astype(o_ref.dtype)


def _dilated(P3, pos3, invf_row):
    B, S, _ = P3.shape

    def head_spec(base, g):
        return pl.BlockSpec((None, S, HEAD_DIM), lambda b, h: (b, 0, base // HEAD_DIM + g * DIL_HPG + h))

    ngr = len(DIL_GROUPS)
    in_specs = ([head_spec(COL_BQ, g) for g in range(ngr)] + [head_spec(COL_BK, g) for g in range(ngr)]
                + [head_spec(COL_BV, g) for g in range(ngr)]
                + [pl.BlockSpec((None, S, 1), lambda b, h: (b, 0, 0)),
                   pl.BlockSpec((1, HEAD_DIM), lambda b, h: (0, 0))])
    big = pltpu.VMEM((S, HEAD_DIM), F32)
    return pl.pallas_call(
        _dilated_kernel,
        grid=(B, DIL_HPG),
        in_specs=in_specs,
        out_specs=pl.BlockSpec((None, S, HEAD_DIM), lambda b, h: (b, 0, h)),
        out_shape=jax.ShapeDtypeStruct((B, S, DIL_OUT), BF16),
        scratch_shapes=[big, big] + [pltpu.VMEM((ngr, S, HEAD_DIM), F32)] * 5,
        compiler_params=_cparams(2),
        name="dilated",
    )(*([P3] * 9), pos3, invf_row)


def _mix_kernel(oa_ref, ob_ref, ga_ref, gb_ref, x_ref, wa_ref, wb_ref, wm_ref, g_ref, b_ref, o_ref):
    subs = [slice(r, r + ROW_SUB) for r in range(0, x_ref.shape[0], ROW_SUB)]
    ya = [_dot(oa_ref[r, :], wa_ref[...]) for r in subs]
    yb = [_dot(ob_ref[r, :], wb_ref[...]) for r in subs]
    hs = [(_sigmoid(ga_ref[r, :].astype(F32)) * a + _sigmoid(gb_ref[r, :].astype(F32)) * b).astype(BF16)
          for r, a, b in zip(subs, ya, yb)]
    mix = [_dot(h, wm_ref[...]) for h in hs]
    for r, m in zip(subs, mix):
        o_ref[r, :] = _layer_norm(ALPHA * x_ref[r, :] + m, g_ref[...], b_ref[...])


def _const_spec(shape):
    return pl.BlockSpec(shape, lambda *_: (0,) * len(shape), pipeline_mode=pl.Buffered(1))


def _mix(oa, ob, P, x2d, wa, wb, wm, g, b, tm):
    T = x2d.shape[0]
    row = lambda w, j=0: pl.BlockSpec((tm, w), lambda i: (i, j))
    return pl.pallas_call(
        _mix_kernel,
        grid=(T // tm,),
        in_specs=[row(DN_W), row(DIL_OUT), row(D_MODEL, COL_GA // D_MODEL), row(D_MODEL, COL_GB // D_MODEL),
                  row(D_MODEL), _const_spec(wa.shape), _const_spec(wb.shape), _const_spec(wm.shape),
                  _const_spec(g.shape), _const_spec(b.shape)],
        out_specs=row(D_MODEL),
        out_shape=jax.ShapeDtypeStruct((T, D_MODEL), F32),
        compiler_params=_cparams(1),
        name="mix",
    )(oa, ob, P, P, x2d, wa, wb, wm, g, b)


def _memkv_kernel(m_ref, w_ref, o_ref):
    o_ref[...] = _dot(m_ref[...].astype(BF16), w_ref[...]).astype(o_ref.dtype)


def _mem_kv(mem2d, wkv, tm):
    R = mem2d.shape[0]
    return pl.pallas_call(
        _memkv_kernel,
        grid=(R // tm,),
        in_specs=[pl.BlockSpec((tm, D_MODEL), lambda i: (i, 0)), _const_spec(wkv.shape)],
        out_specs=pl.BlockSpec((tm, 2 * D_MODEL), lambda i: (i, 0)),
        out_shape=jax.ShapeDtypeStruct((R, 2 * D_MODEL), BF16),
        compiler_params=_cparams(1),
        name="mem_kv",
    )(mem2d, wkv)


def _xattn_kernel(x_ref, kv_ref, wq_ref, wo_ref, g_ref, b_ref, o_ref):
    tm = x_ref.shape[0]
    subs = [slice(r, r + ROW_SUB) for r in range(0, tm, ROW_SUB)]
    scale = MEM_HEAD_DIM ** -0.5
    heads = [(slice(h * MEM_HEAD_DIM, (h + 1) * MEM_HEAD_DIM),
              slice(D_MODEL + h * MEM_HEAD_DIM, D_MODEL + (h + 1) * MEM_HEAD_DIM)) for h in range(MEM_HEADS)]
    xs = [x_ref[r, :] for r in subs]
    qs = [_dot(x.astype(BF16), wq_ref[...]) for x in xs]
    ss = [[_dot_nt(q[:, kc].astype(BF16), kv_ref[:, kc]) * scale for kc, _ in heads] for q in qs]
    ps = []
    for s_sub in ss:
        row = []
        for s in s_sub:
            p = jnp.exp(s - jnp.max(s, axis=-1, keepdims=True))
            row.append((p * (1.0 / jnp.sum(p, axis=-1, keepdims=True))).astype(BF16))
        ps.append(row)
    os = [jnp.concatenate([_dot(p, kv_ref[:, vc]).astype(BF16) for p, (_, vc) in zip(p_sub, heads)], axis=1)
          for p_sub in ps]
    cross = [_dot(o, wo_ref[...]) for o in os]
    for r, x, c in zip(subs, xs, cross):
        o_ref[r, :] = _layer_norm(ALPHA * x + c, g_ref[...], b_ref[...])


def _xattn(x3, kv3, wq, wo, g, b, tm):
    B, S, _ = x3.shape
    M = kv3.shape[1]
    return pl.pallas_call(
        _xattn_kernel,
        grid=(B, S // tm),
        in_specs=[pl.BlockSpec((None, tm, D_MODEL), lambda bi, i: (bi, i, 0)),
                  pl.BlockSpec((None, M, 2 * D_MODEL), lambda bi, i: (bi, 0, 0)),
                  _const_spec(wq.shape), _const_spec(wo.shape), _const_spec(g.shape), _const_spec(b.shape)],
        out_specs=pl.BlockSpec((None, tm, D_MODEL), lambda bi, i: (bi, i, 0)),
        out_shape=jax.ShapeDtypeStruct((B, S, D_MODEL), F32),
        compiler_params=_cparams(2),
        name="xattn",
    )(x3, kv3, wq, wo, g, b)


def _ffn_kernel(x_ref, w13_ref, w2_ref, g_ref, b_ref, o_ref):
    x = x_ref[...]
    xb = x.astype(BF16)
    acc = jnp.zeros(x.shape, F32)
    for j in range(D_FF // FF_CHUNK):
        ca = slice(j * FF_CHUNK, (j + 1) * FF_CHUNK)
        cb = slice(D_FF + j * FF_CHUNK, D_FF + (j + 1) * FF_CHUNK)
        a = _dot(xb, w13_ref[:, ca])
        h = (a * _sigmoid(a)) * _dot(xb, w13_ref[:, cb])
        acc = acc + _dot(h.astype(BF16), w2_ref[ca, :])
    o_ref[...] = _layer_norm(ALPHA * x + acc, g_ref[...], b_ref[...])


def _ffn(x2d, w13, w2, g, b, tm):
    T = x2d.shape[0]
    return pl.pallas_call(
        _ffn_kernel,
        grid=(T // tm,),
        in_specs=[pl.BlockSpec((tm, D_MODEL), lambda i: (i, 0)),
                  _const_spec(w13.shape), _const_spec(w2.shape), _const_spec(g.shape), _const_spec(b.shape)],
        out_specs=pl.BlockSpec((tm, D_MODEL), lambda i: (i, 0)),
        out_shape=jax.ShapeDtypeStruct((T, D_MODEL), F32),
        compiler_params=_cparams(1),
        name="ffn",
    )(x2d, w13, w2, g, b)


def _relayout_w_in(w):
    sizes = (3 * DN_W, DN_W, DN_HEADS, DN_HEADS, DIL_W, DIL_W, DIL_W, D_MODEL, D_MODEL)
    offs = [0]
    for sz in sizes:
        offs.append(offs[-1] + sz)
    wt = w.T
    part = lambda i: wt[offs[i]:offs[i + 1], :]
    used = COL_GATES + 2 * DN_HEADS
    pad = jnp.zeros((NP - used, w.shape[0]), w.dtype)
    rows = jnp.concatenate([part(0), part(1), part(7), part(8), part(4), part(5), part(6),
                            part(2), part(3), pad], axis=0)
    return rows.astype(BF16).T


def _pad_row(v, fill=0.0):
    return jnp.concatenate([v.astype(F32), jnp.full((LANES - v.shape[0],), fill, F32)])[None, :]


def kernel(x, mem, positions, w_in, conv_w, a_log, dt_bias, dn_norm_g, w_br_a, w_br_b, w_mix_out,
           ln1_g, ln1_b, w_xq, w_xkv, w_xo, ln2_g, ln2_b, w_ffn13, w_ffn2, ln3_g, ln3_b):
    B, S, D = x.shape
    T = B * S
    M = mem.shape[1]
    assert D == D_MODEL and w_in.shape[0] == 1 and S % (DIL_BLOCK * DIL_GROUPS[-1][1]) == 0
    x2d = x.reshape(T, D)
    bf = lambda w: w[0].astype(BF16)
    row = lambda v: v[0][None, :].astype(F32)

    P, G = _in_proj(x2d, _relayout_w_in(w_in[0]), conv_w[0], tm=S)
    P3 = P.reshape(B, S, NP)

    o_a = _deltanet(P3, G.reshape(B, S, LANES), _pad_row(a_log[0]), _pad_row(dt_bias[0]), row(dn_norm_g))

    inv_freq = ROPE_THETA ** (-jnp.arange(0, HEAD_DIM, 2, dtype=F32) / HEAD_DIM)
    invf_row = jnp.concatenate([inv_freq, inv_freq])[None, :]
    o_b = _dilated(P3, positions.reshape(B, S, 1), invf_row)

    x1 = _mix(o_a.reshape(T, DN_W), o_b.reshape(T, DIL_OUT), P, x2d, bf(w_br_a), bf(w_br_b), bf(w_mix_out),
              row(ln1_g), row(ln1_b), tm=2 * ROW_SUB)

    kv = _mem_kv(mem.reshape(B * M, D), bf(w_xkv), tm=512)
    x2 = _xattn(x1.reshape(B, S, D), kv.reshape(B, M, 2 * D), bf(w_xq), bf(w_xo), row(ln2_g), row(ln2_b), tm=2 * ROW_SUB)

    out = _ffn(x2.reshape(T, D), bf(w_ffn13), bf(w_ffn2), row(ln3_g), row(ln3_b), tm=512)
    return out.reshape(B, S, D)
```

```python
import functools
import math

import jax
import jax.numpy as jnp
from jax import lax
from jax.experimental import pallas as pl
from jax.experimental.pallas import tpu as pltpu

F32 = jnp.float32
BF16 = jnp.bfloat16

LANES = 128
D_MODEL = 1024
DN_HEADS = 8
HEAD_DIM = 128
DN_W = DN_HEADS * HEAD_DIM
DN_CONV = 4
DN_BLOCK = 128
DN_HPG = 4
INV_BASE = 16
DN_PART_BLOCKS = (2, 5, 5, 4)
DIL_GROUPS = ((128, 1), (512, 4), (2048, 16))
DIL_HPG = 4
DIL_W = len(DIL_GROUPS) * DIL_HPG * HEAD_DIM
DIL_OUT = DIL_HPG * HEAD_DIM
DIL_BLOCK = 128
ROPE_THETA = 10000.0
MEM_HEADS = 4
MEM_HEAD_DIM = D_MODEL // MEM_HEADS
D_FF = 2816
FF_CHUNK = 256
ROW_SUB = 512
ALPHA = 2.0 ** 0.25
LN_EPS = 1e-5
RMS_EPS = 1e-6
LOG2_E = 1.4426950408889634

COL_QKV = 0
COL_Z = 3 * DN_W
COL_GA = COL_Z + DN_W
COL_GB = COL_GA + D_MODEL
COL_BQ = COL_GB + D_MODEL
COL_BK = COL_BQ + DIL_W
COL_BV = COL_BK + DIL_W
COL_GATES = COL_BV + DIL_W
PROJ_TN = 1024
PROJ_CONV_ROWS = 512
NP = 11264

VMEM_LIMIT = 52 * 1024 * 1024
DN_VMEM_LIMIT = 58 * 1024 * 1024


def _cparams(n_axes, vmem_limit=VMEM_LIMIT):
    return pltpu.CompilerParams(dimension_semantics=("arbitrary",) * n_axes,
                                vmem_limit_bytes=vmem_limit)


def _dot(a, b):
    return jnp.dot(a, b, preferred_element_type=F32)


def _dot_nt(a, b):
    return lax.dot_general(a, b, (((1,), (1,)), ((), ())), preferred_element_type=F32)


def _sigmoid(x):
    return 1.0 / (1.0 + jnp.exp2(x * -LOG2_E))


def _silu(x):
    return x * _sigmoid(x)


def _layer_norm(y, g, b):
    mu = jnp.mean(y, axis=-1, keepdims=True)
    yc = y - mu
    var = jnp.mean(yc * yc, axis=-1, keepdims=True)
    return yc * lax.rsqrt(var + LN_EPS) * g + b


def _proj_kernel(x_ref, w_ref, cw_ref, o_ref, gates_ref, xb_s):
    S = x_ref.shape[0]
    j = pl.program_id(1)
    n_conv = 3 * DN_W // PROJ_TN

    @pl.when(j == 0)
    def _():
        xb_s[...] = x_ref[...].astype(BF16)

    @pl.when(j < n_conv)
    def _():
        halo = 8
        w = w_ref[...]
        cw = cw_ref[...]
        tail = jnp.zeros((halo, PROJ_TN), F32)
        n_chunks = S // PROJ_CONV_ROWS
        chunk = lambda c: _dot(xb_s[c * PROJ_CONV_ROWS:(c + 1) * PROJ_CONV_ROWS, :], w)
        nxt = chunk(0)
        for c in range(n_chunks):
            rows = slice(c * PROJ_CONV_ROWS, (c + 1) * PROJ_CONV_ROWS)
            acc = nxt
            if c + 1 < n_chunks:
                nxt = chunk(c + 1)
            win = jnp.concatenate([tail, acc], axis=0)
            tail = acc[PROJ_CONV_ROWS - halo:, :]
            y = acc * cw[DN_CONV - 1:DN_CONV, :]
            for t in range(DN_CONV - 1):
                y = y + pltpu.roll(win, DN_CONV - 1 - t, 0)[halo:, :] * cw[t:t + 1, :]
            o_ref[rows, :] = _silu(y).astype(o_ref.dtype)

    @pl.when(j == COL_Z // PROJ_TN)
    def _():
        w = w_ref[...]
        n_chunks = S // PROJ_CONV_ROWS
        chunk = lambda c: _dot(xb_s[c * PROJ_CONV_ROWS:(c + 1) * PROJ_CONV_ROWS, :], w)
        nxt = chunk(0)
        for c in range(n_chunks):
            acc = nxt
            if c + 1 < n_chunks:
                nxt = chunk(c + 1)
            o_ref[c * PROJ_CONV_ROWS:(c + 1) * PROJ_CONV_ROWS, :] = _silu(acc).astype(o_ref.dtype)

    @pl.when(j > COL_Z // PROJ_TN)
    def _():
        acc = _dot(xb_s[...], w_ref[...])
        o_ref[...] = acc.astype(o_ref.dtype)

        @pl.when(j == COL_GATES // PROJ_TN)
        def _():
            lo = COL_GATES % PROJ_TN
            gates_ref[...] = acc[:, lo:lo + LANES]


def _in_proj(x2d, wb, conv_w, tm):
    T = x2d.shape[0]
    n_conv = 3 * DN_W // PROJ_TN
    assert COL_QKV == 0 and COL_Z == 3 * DN_W and DN_W % PROJ_TN == 0
    return pl.pallas_call(
        _proj_kernel,
        grid=(T // tm, NP // PROJ_TN),
        in_specs=[pl.BlockSpec((tm, D_MODEL), lambda i, j: (i, 0)),
                  pl.BlockSpec((D_MODEL, PROJ_TN), lambda i, j: (0, j)),
                  pl.BlockSpec((DN_CONV, PROJ_TN), lambda i, j: (0, jnp.minimum(j, n_conv - 1)))],
        out_specs=[pl.BlockSpec((tm, PROJ_TN), lambda i, j: (i, j)),
                   pl.BlockSpec((tm, LANES), lambda i, j: (i, 0))],
        out_shape=[jax.ShapeDtypeStruct((T, NP), BF16), jax.ShapeDtypeStruct((T, LANES), F32)],
        scratch_shapes=[pltpu.VMEM((tm, D_MODEL), BF16)],
        compiler_params=_cparams(2),
        name="in_proj",
    )(x2d, wb, conv_w)


def _l2norm(x):
    return x * lax.rsqrt(jnp.sum(x * x, axis=-1, keepdims=True) + RMS_EPS)


def _deltanet_kernel(q_ref, k_ref, v_ref, z_ref, gates_ref, alog_ref, dtb_ref, ng_ref, o_ref,
                     gam_s, beta_s, n_s, t_s, p_s, qk_s, rhs_s, mq_s, kdt_s, qd_s, gt_s):
    S = q_ref.shape[0]
    nblk = S // DN_BLOCK
    hg = pl.program_id(1)
    blk = DN_BLOCK
    bc_s, o0_s = n_s, t_s

    @pl.when(hg == 0)
    def _():
        G = gates_ref[...]
        zz = G + dtb_ref[...]
        softplus = jnp.maximum(zz, 0.0) + jnp.log1p(jnp.exp(-jnp.abs(zz)))
        gam = -jnp.exp(alog_ref[...]) * softplus
        rowc = lax.broadcasted_iota(jnp.int32, G.shape, 0) & (blk - 1)
        sh = 1
        while sh < blk:
            gam = gam + jnp.where(rowc >= sh, pltpu.roll(gam, sh, 0), 0.0)
            sh *= 2
        gam_s[...] = gam
        beta_s[...] = _sigmoid(G)

    ri = lax.broadcasted_iota(jnp.int32, (blk, blk), 0)
    ci = lax.broadcasted_iota(jnp.int32, (blk, blk), 1)

    def same_block(b):
        s = int(math.log2(b))
        return (ri >> s) == (ci >> s)

    def setup(c, carry=None):
        rows = pl.ds(pl.multiple_of(c * blk, blk), blk)
        gsl = gam_s[rows, :]
        bsl = beta_s[rows, :]
        for i in range(DN_HPG):
            h = hg * DN_HPG + i
            cols = slice(i * HEAD_DIM, (i + 1) * HEAD_DIM)
            gam_col = jnp.sum(jnp.where(ci == h, gsl, 0.0), axis=1, keepdims=True)
            beta_col = jnp.sum(jnp.where(ci == h + DN_HEADS, bsl, 0.0), axis=1, keepdims=True)
            gb = jnp.broadcast_to(gam_col, (blk, HEAD_DIM))
            kn = _l2norm(k_ref[rows, cols].astype(F32))
            qn = _l2norm(q_ref[rows, cols].astype(F32)) * (HEAD_DIM ** -0.5)
            kb = kn * beta_col
            dec = jnp.exp(jnp.where(ri >= ci, gb - gb.T, -jnp.inf))
            a = _dot_nt(jnp.concatenate([qn, kb], axis=0).astype(BF16), kn.astype(BF16))
            qk_s[i, c] = (a[:blk] * dec).astype(BF16)
            n_s[i, c] = -a[blk:] * jnp.where(ri > ci, dec, 0.0)
            eg = jnp.exp(gb)
            rhs_s[i, c] = jnp.concatenate([v_ref[rows, cols].astype(F32) * beta_col, kb * eg], axis=1).astype(BF16)
            qd_s[i, c] = (qn * eg).astype(BF16)
            gl = jnp.broadcast_to(gb[blk - 1:blk, :], (blk, HEAD_DIM))
            kdt_s[i, c] = (kn * jnp.exp(gl - gb)).T.astype(BF16)
            gt_s[i, c] = jnp.exp(gl[:8, :])
        return carry

    def series_start(i, c):
        d = jnp.where(same_block(INV_BASE), n_s[i, c], 0.0)
        return d, _dot(d.astype(BF16), d.astype(BF16))

    def series_start_done(i, c, r):
        t_s[i, c] = (ri == ci).astype(F32) + r[0]
        p_s[i, c] = r[1].astype(BF16)

    def series_both(i, c):
        pb = p_s[i, c]
        return _dot(jnp.concatenate([t_s[i, c].astype(BF16), pb], axis=0), pb)

    def series_both_done(i, c, r):
        t_s[i, c] = t_s[i, c] + r[:blk]
        p_s[i, c] = r[blk:].astype(BF16)

    def times_p(i, c):
        return _dot(t_s[i, c].astype(BF16), p_s[i, c])

    def add_to_t(i, c, r):
        t_s[i, c] = t_s[i, c] + r

    def to_p(i, c, r):
        p_s[i, c] = r.astype(BF16)

    stages = [(series_start, series_start_done)]
    stages += [(series_both, series_both_done)] * (int(math.log2(INV_BASE)) - 2)
    stages += [(times_p, add_to_t)]
    b = INV_BASE
    while b < blk:
        off = same_block(2 * b) & jnp.logical_not(same_block(b))

        def off_times_t(i, c, off=off):
            return _dot(jnp.where(off, n_s[i, c], 0.0).astype(BF16), t_s[i, c].astype(BF16))

        stages += [(off_times_t, to_p), (times_p, add_to_t)]
        b *= 2

    def solve(i, c):
        return _dot(t_s[i, c].astype(BF16), rhs_s[i, c])

    def solve_done(i, c, r):
        rhs_s[i, c] = r.astype(BF16)

    def cross(i, c):
        return _dot(jnp.concatenate([kdt_s[i, c], qk_s[i, c]], axis=0), rhs_s[i, c])

    def cross_done(i, c, r):
        bc_s[i, c] = r[:blk, :HEAD_DIM]
        o0_s[i, c] = r[blk:, :HEAD_DIM]
        mq_s[i, c] = jnp.concatenate([r[:blk, HEAD_DIM:], qd_s[i, c].astype(F32) - r[blk:, HEAD_DIM:]],
                                     axis=0).astype(BF16)

    stages += [(solve, solve_done), (cross, cross_done)]

    def run_stages(blocks, side_work):
        probs = [(i, c) for c in blocks for i in range(DN_HPG)]
        for k, (issue, finish) in enumerate(stages):
            results = [issue(i, c) for i, c in probs]
            if k < len(side_work):
                side_work[k]()
            for (i, c), r in zip(probs, results):
                finish(i, c, r)

    ng = ng_ref[...]

    def scan_step(c, states):
        rows = pl.ds(pl.multiple_of(c * blk, blk), blk)
        new_states = []
        for i in range(DN_HPG):
            cols = slice(i * HEAD_DIM, (i + 1) * HEAD_DIM)
            st = states[i]
            r = _dot(mq_s[i, c], st.astype(BF16))
            g_tot = jnp.broadcast_to(gt_s[i, c][0:1, :], (HEAD_DIM, HEAD_DIM))
            new_states.append(g_tot * st + bc_s[i, c] - r[:blk])
            o = o0_s[i, c] + r[blk:]
            o = o * lax.rsqrt(jnp.mean(o * o, axis=-1, keepdims=True) + RMS_EPS) * ng
            o_ref[rows, cols] = (o * z_ref[rows, cols].astype(F32)).astype(o_ref.dtype)
        return tuple(new_states)

    assert sum(DN_PART_BLOCKS) == nblk
    starts = [sum(DN_PART_BLOCKS[:p]) for p in range(len(DN_PART_BLOCKS) + 1)]
    parts = [list(range(starts[p], starts[p + 1])) for p in range(len(DN_PART_BLOCKS))]
    lax.fori_loop(0, len(parts[0]), setup, 0)

    state = [tuple(jnp.zeros((HEAD_DIM, HEAD_DIM), F32) for _ in range(DN_HPG))]

    def scan_side(c):
        state[0] = scan_step(c, state[0])

    for p in range(len(parts)):
        side = [functools.partial(setup, c) for c in (parts[p + 1] if p + 1 < len(parts) else [])]
        side += [functools.partial(scan_side, c) for c in (parts[p - 1] if p > 0 else [])]
        assert len(side) <= len(stages)
        run_stages(parts[p], side)
    lax.fori_loop(parts[-1][0], nblk, scan_step, state[0])


def _deltanet(P3, G3, alog_row, dtb_row, ng_row):
    B, S, _ = P3.shape
    W = DN_HPG * HEAD_DIM
    ngrp = DN_HEADS // DN_HPG
    nblk = S // DN_BLOCK

    def col_spec(base):
        return pl.BlockSpec((None, S, W), lambda b, g: (b, 0, base // W + g))

    row_spec = pl.BlockSpec((1, LANES), lambda b, g: (0, 0))
    blk_buf = lambda rows, cols, dt: pltpu.VMEM((DN_HPG, nblk, rows, cols), dt)
    return pl.pallas_call(
        _deltanet_kernel,
        grid=(B, ngrp),
        in_specs=[col_spec(COL_QKV), col_spec(COL_QKV + DN_W), col_spec(COL_QKV + 2 * DN_W), col_spec(COL_Z),
                  pl.BlockSpec((None, S, LANES), lambda b, g: (b, 0, 0), pipeline_mode=pl.Buffered(1)),
                  row_spec, row_spec, row_spec],
        out_specs=pl.BlockSpec((None, S, W), lambda b, g: (b, 0, g)),
        out_shape=jax.ShapeDtypeStruct((B, S, DN_W), BF16),
        scratch_shapes=[pltpu.VMEM((S, LANES), F32), pltpu.VMEM((S, LANES), F32),
                        blk_buf(DN_BLOCK, DN_BLOCK, F32), blk_buf(DN_BLOCK, DN_BLOCK, F32),
                        blk_buf(DN_BLOCK, DN_BLOCK, BF16), blk_buf(DN_BLOCK, DN_BLOCK, BF16),
                        blk_buf(DN_BLOCK, 2 * HEAD_DIM, BF16),
                        blk_buf(2 * DN_BLOCK, HEAD_DIM, BF16),
                        blk_buf(HEAD_DIM, DN_BLOCK, BF16),
                        blk_buf(DN_BLOCK, HEAD_DIM, BF16),
                        blk_buf(8, HEAD_DIM, F32)],
        compiler_params=_cparams(2, DN_VMEM_LIMIT),
        name="deltanet",
    )(P3, P3, P3, P3, G3, alog_row, dtb_row, ng_row)


def _dilated_kernel(q0_ref, q1_ref, q2_ref, k0_ref, k1_ref, k2_ref, v0_ref, v1_ref, v2_ref,
                    pos_ref, invf_ref, o_ref, cos_s, sin_s, qr_s, kr_s, vr_s, og_s, lg_s):
    S = q0_ref.shape[0]
    blk = DIL_BLOCK
    qscale = HEAD_DIM ** -0.5 * LOG2_E

    @pl.when(pl.program_id(1) == 0)
    def _():
        hs, hl = S // 2, HEAD_DIM // 2
        pos = pos_ref[...].astype(F32)
        lane = lax.broadcasted_iota(jnp.int32, (hs, HEAD_DIM), 1)
        low = lane < hl
        ang = jnp.where(low, pos[:hs, :], pos[hs:, :]) * invf_ref[...]
        c, sn = jnp.cos(ang), jnp.sin(ang)
        c_sw, sn_sw = pltpu.roll(c, hl, 1), pltpu.roll(sn, hl, 1)
        cos_s[pl.ds(0, hs), :] = jnp.where(low, c, c_sw)
        cos_s[pl.ds(hs, hs), :] = jnp.where(low, c_sw, c)
        sin_s[pl.ds(0, hs), :] = jnp.where(low, -sn, sn_sw)
        sin_s[pl.ds(hs, hs), :] = jnp.where(low, -sn_sw, sn)

    def rope(x):
        return x * cos_s[...] + pltpu.roll(x, HEAD_DIM // 2, 1) * sin_s[...]

    ri = lax.broadcasted_iota(jnp.int32, (blk, blk), 0)
    ci = lax.broadcasted_iota(jnp.int32, (blk, blk), 1)
    neg = -jnp.inf

    def rows_of(start, dil):
        return pl.ds(start, blk) if dil == 1 else pl.ds(start, blk, stride=dil)

    ones = jnp.ones((blk, HEAD_DIM), BF16)

    def block_group(g, starts, dil, has_prev):
        rows = [rows_of(st, dil) for st in starts]
        q = [qr_s[g, r, :].astype(BF16) for r in rows]
        s = [_dot_nt(q[j], kr_s[g, r, :].astype(BF16)) for j, r in enumerate(rows)]
        if has_prev:
            prows = [rows_of(st - blk * dil, dil) for st in starts]
            sp = [_dot_nt(q[j], kr_s[g, r, :].astype(BF16)) for j, r in enumerate(prows)]
        ps, pps, ms = [], [], []
        for j in range(len(starts)):
            sj = jnp.where(ri >= ci, s[j], neg)
            if has_prev:
                spj = jnp.where(ci >= ri, sp[j], neg)
                m = jnp.max(jnp.maximum(sj, spj), axis=-1, keepdims=True)
                pps.append(jnp.exp2(spj - m).astype(BF16))
            else:
                m = jnp.max(sj, axis=-1, keepdims=True)
            ps.append(jnp.exp2(sj - m).astype(BF16))
            ms.append(m)
        for j, r in enumerate(rows):
            ol = _dot(ps[j], jnp.concatenate([vr_s[g, r, :].astype(BF16), ones], axis=1))
            if has_prev:
                ol = ol + _dot(pps[j], jnp.concatenate([vr_s[g, prows[j], :].astype(BF16), ones], axis=1))
            l = ol[:, HEAD_DIM:]
            og_s[g, r, :] = ol[:, :HEAD_DIM] * (1.0 / l)
            lg_s[g, r, :] = ms[j] * (1.0 / LOG2_E) + jnp.log(l)

    q_refs = (q0_ref, q1_ref, q2_ref)
    k_refs = (k0_ref, k1_ref, k2_ref)
    v_refs = (v0_ref, v1_ref, v2_ref)
    for g in range(len(DIL_GROUPS)):
        qr_s[g] = rope(q_refs[g][...].astype(F32)) * qscale
        kr_s[g] = rope(k_refs[g][...].astype(F32))
        vr_s[g] = v_refs[g][...].astype(F32)
    for g, (window, dil) in enumerate(DIL_GROUPS):
        assert window // dil == blk
        nb = S // dil // blk
        block_group(g, list(range(dil)), dil, False)
        if nb > 1:
            block_group(g, [n * blk * dil + r for n in range(1, nb) for r in range(dil)], dil, True)

    l0, l1, l2 = lg_s[0], lg_s[1], lg_s[2]
    m = jnp.maximum(jnp.maximum(l0, l1), l2)
    e0, e1, e2 = jnp.exp(l0 - m), jnp.exp(l1 - m), jnp.exp(l2 - m)
    inv = 1.0 / (e0 + e1 + e2)
    o_ref[...] = ((e0 * inv) * og_s[0] + (e1 * inv) * og_s[1] + (e2 * inv) * og_s[2]).astype(o_ref.dtype)


def _dilated(P3, pos3, invf_row):
    B, S, _ = P3.shape

    def head_spec(base, g):
        return pl.BlockSpec((None, S, HEAD_DIM), lambda b, h: (b, 0, base // HEAD_DIM + g * DIL_HPG + h))

    ngr = len(DIL_GROUPS)
    in_specs = ([head_spec(COL_BQ, g) for g in range(ngr)] + [head_spec(COL_BK, g) for g in range(ngr)]
                + [head_spec(COL_BV, g) for g in range(ngr)]
                + [pl.BlockSpec((None, S, 1), lambda b, h: (b, 0, 0)),
                   pl.BlockSpec((1, HEAD_DIM), lambda b, h: (0, 0))])
    big = pltpu.VMEM((S, HEAD_DIM), F32)
    return pl.pallas_call(
        _dilated_kernel,
        grid=(B, DIL_HPG),
        in_specs=in_specs,
        out_specs=pl.BlockSpec((None, S, HEAD_DIM), lambda b, h: (b, 0, h)),
        out_shape=jax.ShapeDtypeStruct((B, S, DIL_OUT), BF16),
        scratch_shapes=[big, big] + [pltpu.VMEM((ngr, S, HEAD_DIM), F32)] * 5,
        compiler_params=_cparams(2),
        name="dilated",
    )(*([P3] * 9), pos3, invf_row)


def _mix_kernel(oa_ref, ob_ref, ga_ref, gb_ref, x_ref, wa_ref, wb_ref, wm_ref, g_ref, b_ref, o_ref):
    subs = [slice(r, r + ROW_SUB) for r in range(0, x_ref.shape[0], ROW_SUB)]
    ya = [_dot(oa_ref[r, :], wa_ref[...]) for r in subs]
    yb = [_dot(ob_ref[r, :], wb_ref[...]) for r in subs]
    hs = [(_sigmoid(ga_ref[r, :].astype(F32)) * a + _sigmoid(gb_ref[r, :].astype(F32)) * b).astype(BF16)
          for r, a, b in zip(subs, ya, yb)]
    mix = [_dot(h, wm_ref[...]) for h in hs]
    for r, m in zip(subs, mix):
        o_ref[r, :] = _layer_norm(ALPHA * x_ref[r, :] + m, g_ref[...], b_ref[...])


def _const_spec(shape):
    return pl.BlockSpec(shape, lambda *_: (0,) * len(shape), pipeline_mode=pl.Buffered(1))


def _mix(oa, ob, P, x2d, wa, wb, wm, g, b, tm):
    T = x2d.shape[0]
    row = lambda w, j=0: pl.BlockSpec((tm, w), lambda i: (i, j))
    return pl.pallas_call(
        _mix_kernel,
        grid=(T // tm,),
        in_specs=[row(DN_W), row(DIL_OUT), row(D_MODEL, COL_GA // D_MODEL), row(D_MODEL, COL_GB // D_MODEL),
                  row(D_MODEL), _const_spec(wa.shape), _const_spec(wb.shape), _const_spec(wm.shape),
                  _const_spec(g.shape), _const_spec(b.shape)],
        out_specs=row(D_MODEL),
        out_shape=jax.ShapeDtypeStruct((T, D_MODEL), F32),
        compiler_params=_cparams(1),
        name="mix",
    )(oa, ob, P, P, x2d, wa, wb, wm, g, b)


def _memkv_kernel(m_ref, w_ref, o_ref):
    o_ref[...] = _dot(m_ref[...].astype(BF16), w_ref[...]).astype(o_ref.dtype)


def _mem_kv(mem2d, wkv, tm):
    R = mem2d.shape[0]
    return pl.pallas_call(
        _memkv_kernel,
        grid=(R // tm,),
        in_specs=[pl.BlockSpec((tm, D_MODEL), lambda i: (i, 0)), _const_spec(wkv.shape)],
        out_specs=pl.BlockSpec((tm, 2 * D_MODEL), lambda i: (i, 0)),
        out_shape=jax.ShapeDtypeStruct((R, 2 * D_MODEL), BF16),
        compiler_params=_cparams(1),
        name="mem_kv",
    )(mem2d, wkv)


def _xattn_kernel(x_ref, kv_ref, wq_ref, wo_ref, g_ref, b_ref, o_ref):
    tm = x_ref.shape[0]
    subs = [slice(r, r + ROW_SUB) for r in range(0, tm, ROW_SUB)]
    scale = MEM_HEAD_DIM ** -0.5
    heads = [(slice(h * MEM_HEAD_DIM, (h + 1) * MEM_HEAD_DIM),
              slice(D_MODEL + h * MEM_HEAD_DIM, D_MODEL + (h + 1) * MEM_HEAD_DIM)) for h in range(MEM_HEADS)]
    xs = [x_ref[r, :] for r in subs]
    qs = [_dot(x.astype(BF16), wq_ref[...]) for x in xs]
    ss = [[_dot_nt(q[:, kc].astype(BF16), kv_ref[:, kc]) * scale for kc, _ in heads] for q in qs]
    ps = []
    for s_sub in ss:
        row = []
        for s in s_sub:
            p = jnp.exp(s - jnp.max(s, axis=-1, keepdims=True))
            row.append((p * (1.0 / jnp.sum(p, axis=-1, keepdims=True))).astype(BF16))
        ps.append(row)
    os = [jnp.concatenate([_dot(p, kv_ref[:, vc]).astype(BF16) for p, (_, vc) in zip(p_sub, heads)], axis=1)
          for p_sub in ps]
    cross = [_dot(o, wo_ref[...]) for o in os]
    for r, x, c in zip(subs, xs, cross):
        o_ref[r, :] = _layer_norm(ALPHA * x + c, g_ref[...], b_ref[...])


def _xattn(x3, kv3, wq, wo, g, b, tm):
    B, S, _ = x3.shape
    M = kv3.shape[1]
    return pl.pallas_call(
        _xattn_kernel,
        grid=(B, S // tm),
        in_specs=[pl.BlockSpec((None, tm, D_MODEL), lambda bi, i: (bi, i, 0)),
                  pl.BlockSpec((None, M, 2 * D_MODEL), lambda bi, i: (bi, 0, 0)),
                  _const_spec(wq.shape), _const_spec(wo.shape), _const_spec(g.shape), _const_spec(b.shape)],
        out_specs=pl.BlockSpec((None, tm, D_MODEL), lambda bi, i: (bi, i, 0)),
        out_shape=jax.ShapeDtypeStruct((B, S, D_MODEL), F32),
        compiler_params=_cparams(2),
        name="xattn",
    )(x3, kv3, wq, wo, g, b)


def _ffn_kernel(x_ref, w13_ref, w2_ref, g_ref, b_ref, o_ref):
    for r in [slice(r0, r0 + ROW_SUB) for r0 in range(0, x_ref.shape[0], ROW_SUB)]:
        x = x_ref[r, :]
        xb = x.astype(BF16)
        acc = jnp.zeros(x.shape, F32)
        for j in range(D_FF // FF_CHUNK):
            ca = slice(j * FF_CHUNK, (j + 1) * FF_CHUNK)
            cb = slice(D_FF + j * FF_CHUNK, D_FF + (j + 1) * FF_CHUNK)
            a = _dot(xb, w13_ref[:, ca])
            h = (a * _sigmoid(a)) * _dot(xb, w13_ref[:, cb])
            acc = acc + _dot(h.astype(BF16), w2_ref[ca, :])
        o_ref[r, :] = _layer_norm(ALPHA * x + acc, g_ref[...], b_ref[...])


def _ffn(x2d, w13, w2, g, b, tm):
    T = x2d.shape[0]
    return pl.pallas_call(
        _ffn_kernel,
        grid=(T // tm,),
        in_specs=[pl.BlockSpec((tm, D_MODEL), lambda i: (i, 0)),
                  _const_spec(w13.shape), _const_spec(w2.shape), _const_spec(g.shape), _const_spec(b.shape)],
        out_specs=pl.BlockSpec((tm, D_MODEL), lambda i: (i, 0)),
        out_shape=jax.ShapeDtypeStruct((T, D_MODEL), F32),
        compiler_params=_cparams(1),
        name="ffn",
    )(x2d, w13, w2, g, b)


def _relayout_w_in(w):
    sizes = (3 * DN_W, DN_W, DN_HEADS, DN_HEADS, DIL_W, DIL_W, DIL_W, D_MODEL, D_MODEL)
    offs = [0]
    for sz in sizes:
        offs.append(offs[-1] + sz)
    wt = w.T
    part = lambda i: wt[offs[i]:offs[i + 1], :]
    used = COL_GATES + 2 * DN_HEADS
    pad = jnp.zeros((NP - used, w.shape[0]), w.dtype)
    rows = jnp.concatenate([part(0), part(1), part(7), part(8), part(4), part(5), part(6),
                            part(2), part(3), pad], axis=0)
    return rows.astype(BF16).T


def _pad_row(v, fill=0.0):
    return jnp.concatenate([v.astype(F32), jnp.full((LANES - v.shape[0],), fill, F32)])[None, :]


def kernel(x, mem, positions, w_in, conv_w, a_log, dt_bias, dn_norm_g, w_br_a, w_br_b, w_mix_out,
           ln1_g, ln1_b, w_xq, w_xkv, w_xo, ln2_g, ln2_b, w_ffn13, w_ffn2, ln3_g, ln3_b):
    B, S, D = x.shape
    T = B * S
    M = mem.shape[1]
    assert D == D_MODEL and w_in.shape[0] == 1 and S % (DIL_BLOCK * DIL_GROUPS[-1][1]) == 0
    x2d = x.reshape(T, D)
    bf = lambda w: w[0].astype(BF16)
    row = lambda v: v[0][None, :].astype(F32)

    P, G = _in_proj(x2d, _relayout_w_in(w_in[0]), conv_w[0], tm=S)
    P3 = P.reshape(B, S, NP)

    o_a = _deltanet(P3, G.reshape(B, S, LANES), _pad_row(a_log[0]), _pad_row(dt_bias[0]), row(dn_norm_g))

    inv_freq = ROPE_THETA ** (-jnp.arange(0, HEAD_DIM, 2, dtype=F32) / HEAD_DIM)
    invf_row = jnp.concatenate([inv_freq, inv_freq])[None, :]
    o_b = _dilated(P3, positions.reshape(B, S, 1), invf_row)

    x1 = _mix(o_a.reshape(T, DN_W), o_b.reshape(T, DIL_OUT), P, x2d, bf(w_br_a), bf(w_br_b), bf(w_mix_out),
              row(ln1_g), row(ln1_b), tm=2 * ROW_SUB)

    kv = _mem_kv(mem.reshape(B * M, D), bf(w_xkv), tm=512)
    x2 = _xattn(x1.reshape(B, S, D), kv.reshape(B, M, 2 * D), bf(w_xq), bf(w_xo), row(ln2_g), row(ln2_b), tm=2 * ROW_SUB)

    out = _ffn(x2.reshape(T, D), bf(w_ffn13), bf(w_ffn2), row(ln3_g), row(ln3_b), tm=2 * ROW_SUB)
    return out.reshape(B, S, D)
```
